```python
import math
import jax, jax.numpy as jnp
from jax import lax
import numpy as np

D_MODEL = 1024
BATCH = 4
SEQ = 4096
DEPTH = 4

N_A_LAYERS = max(1, DEPTH // 2)
N_B_LAYERS = DEPTH - N_A_LAYERS

DIFF_HEADS = 8
DIFF_QK_DIM = 64
DIFF_V_DIM = 2 * DIFF_QK_DIM
T5_BUCKETS = 32
T5_MAX_DISTANCE = 128
MLA_HEADS = 16
MLA_QK_NOPE = 128
MLA_QK_ROPE = 64
MLA_V_DIM = 128
MLA_Q_LORA = 512
MLA_KV_LORA = 256
ROPE_BASE = 10000.0
D_FF = 2816
CONV_WIDTH = 3
PLE_DIM = 256
Q_BLOCK = 128
RMS_EPS = 1e-6
NEG_INF = -1e30
POS_OFFSET_MAX = 1024

kernel_name = 'yoco_diffattn_mla_convffn_hybrid'


def rms_norm(x, gain):
    xf = x.astype(jnp.float32)
    y = xf * lax.rsqrt(jnp.mean(xf * xf, axis=-1, keepdims=True) + RMS_EPS)
    return (y * gain.astype(jnp.float32)).astype(x.dtype)


def rotary(x, pos):
    d = x.shape[-1]
    half = d // 2
    inv_freq = jnp.exp(-math.log(ROPE_BASE) * jnp.arange(half, dtype=jnp.float32) * (2.0 / d))
    ang = pos.astype(jnp.float32)[..., None] * inv_freq
    if x.ndim == 4:
        ang = ang[:, :, None, :]
    cos, sin = jnp.cos(ang), jnp.sin(ang)
    xf = x.astype(jnp.float32)
    x1, x2 = xf[..., :half], xf[..., half:]
    return jnp.concatenate([x1 * cos - x2 * sin, x1 * sin + x2 * cos], axis=-1).astype(x.dtype)


def t5_bucket(dist):
    n = jnp.maximum(dist, 0)
    max_exact = T5_BUCKETS // 2
    log_ratio = jnp.log(jnp.maximum(n, 1).astype(jnp.float32) / max_exact) / math.log(T5_MAX_DISTANCE / max_exact)
    large = max_exact + (log_ratio * (T5_BUCKETS - max_exact)).astype(jnp.int32)
    large = jnp.minimum(large, T5_BUCKETS - 1)
    return jnp.where(n < max_exact, n, large)


def to_blocks(t):
    b, s = t.shape[:2]
    return t.reshape((b, s // Q_BLOCK, Q_BLOCK) + t.shape[2:]).swapaxes(0, 1)


def from_blocks(t):
    nb, b, q = t.shape[:3]
    return t.swapaxes(0, 1).reshape((b, nb * q) + t.shape[3:])


def causal_mask(blk_idx, seq):
    q_idx = blk_idx * Q_BLOCK + jnp.arange(Q_BLOCK)
    return jnp.arange(seq)[None, :] <= q_idx[:, None]


def diff_attention(h, pos, w_qkv, q_gain, k_gain, lam_q1, lam_k1, lam_q2, lam_k2, sub_gain, w_o, rel_table, lam_init):
    b, s, _ = h.shape
    qk_w = DIFF_HEADS * 2 * DIFF_QK_DIM
    qkv = h @ w_qkv
    q = rms_norm(qkv[..., :qk_w].reshape(b, s, DIFF_HEADS, 2, DIFF_QK_DIM), q_gain)
    k = rms_norm(qkv[..., qk_w:2 * qk_w].reshape(b, s, DIFF_HEADS, 2, DIFF_QK_DIM), k_gain)
    v = qkv[..., 2 * qk_w:].reshape(b, s, DIFF_HEADS, DIFF_V_DIM)
    f32 = jnp.float32
    lam = (jnp.exp(jnp.sum(lam_q1.astype(f32) * lam_k1.astype(f32)))
           - jnp.exp(jnp.sum(lam_q2.astype(f32) * lam_k2.astype(f32))) + lam_init)
    scale = DIFF_QK_DIM ** -0.5
    nb = s // Q_BLOCK

    def block(args):
        i, q_blk, p_blk = args
        sc = jnp.einsum('bqhcd,bkhcd->bhcqk', q_blk, k).astype(f32) * scale
        bucket = t5_bucket(p_blk[:, :, None] - pos[:, None, :])
        bias = jnp.take(rel_table, bucket, axis=0).astype(f32)
        bias = bias.transpose(0, 3, 1, 2)[:, :, None]
        sc = jnp.where(causal_mask(i, s), sc + bias, NEG_INF)
        a = jax.nn.softmax(sc, axis=-1)
        wts = a[:, :, 0] - lam * a[:, :, 1]
        return jnp.einsum('bhqk,bkhv->bqhv', wts.astype(v.dtype), v)

    o = lax.map(block, (jnp.arange(nb), to_blocks(q), to_blocks(pos)))
    o = from_blocks(o)
    o = rms_norm(o, sub_gain) * (1.0 - lam_init)
    return o.reshape(b, s, DIFF_HEADS * DIFF_V_DIM) @ w_o


def shared_latent_kv(h, pos, kv_norm, w_dkv, ckv_norm, w_ukv, k_nope_norm, k_pe_norm):
    b, s, _ = h.shape
    hn = rms_norm(h, kv_norm)
    ckv_full = hn @ w_dkv
    c_kv = rms_norm(ckv_full[..., :MLA_KV_LORA], ckv_norm)
    k_pe = rotary(rms_norm(ckv_full[..., MLA_KV_LORA:], k_pe_norm), pos)
    kv = (c_kv @ w_ukv).reshape(b, s, MLA_HEADS, MLA_QK_NOPE + MLA_V_DIM)
    k_nope = rms_norm(kv[..., :MLA_QK_NOPE], k_nope_norm)
    v = kv[..., MLA_QK_NOPE:]
    return k_nope, k_pe, v


def mla_attention(h, pos, w_dq, cq_norm, w_uq, q_nope_norm, q_pe_norm, w_o, k_nope, k_pe, v):
    b, s, _ = h.shape
    c_q = rms_norm(h @ w_dq, cq_norm)
    q = (c_q @ w_uq).reshape(b, s, MLA_HEADS, MLA_QK_NOPE + MLA_QK_ROPE)
    q_nope = rms_norm(q[..., :MLA_QK_NOPE], q_nope_norm)
    q_pe = rotary(rms_norm(q[..., MLA_QK_NOPE:], q_pe_norm), pos)
    scale = (MLA_QK_NOPE + MLA_QK_ROPE) ** -0.5
    nb = s // Q_BLOCK

    def block(args):
        i, qn, qp = args
        sc = (jnp.einsum('bqhd,bkhd->bhqk', qn, k_nope)
              + jnp.einsum('bqhr,bkr->bhqk', qp, k_pe)).astype(jnp.float32) * scale
        sc = jnp.where(causal_mask(i, s), sc, NEG_INF)
        a = jax.nn.softmax(sc, axis=-1)
        return jnp.einsum('bhqk,bkhv->bqhv', a.astype(v.dtype), v)

    o = from_blocks(lax.map(block, (jnp.arange(nb), to_blocks(q_nope), to_blocks(q_pe))))
    return o.reshape(b, s, MLA_HEADS * MLA_V_DIM) @ w_o


def conv_gated_ffn(h, w_in, conv_w, conv_b, w_out):
    s = h.shape[1]
    u = h @ w_in
    u_pad = jnp.pad(u, ((0, 0), (CONV_WIDTH - 1, 0), (0, 0)))
    c = conv_b
    for j in range(CONV_WIDTH):
        c = c + u_pad[:, j:j + s, :] * conv_w[j]
    a, g = jnp.split(c, 2, axis=-1)
    return (jax.nn.silu(g) * a) @ w_out


def per_layer_embedding(h, p_i, norm_g, w_proj, w_gate):
    gate = jax.nn.sigmoid(rms_norm(h, norm_g) @ w_gate)
    return (p_i @ w_proj) * gate


def setup_inputs(seed: int = 0) -> dict:
    key = jax.random.key(seed)
    keys = iter(jax.random.split(key, 48))
    f32 = jnp.float32

    def dense(shape, fan_in):
        return jax.random.normal(next(keys), shape, f32) * fan_in ** -0.5

    def gain(shape):
        return 1.0 + 0.05 * jax.random.normal(next(keys), shape, f32)

    def small(shape, scale):
        return scale * jax.random.normal(next(keys), shape, f32)

    na, nbl = N_A_LAYERS, N_B_LAYERS
    qkv_w = 2 * DIFF_HEADS * 2 * DIFF_QK_DIM + DIFF_HEADS * DIFF_V_DIM
    x = jax.random.normal(next(keys), (BATCH, SEQ, D_MODEL), f32)
    p = jax.random.normal(next(keys), (DEPTH, BATCH, SEQ, PLE_DIM), f32)
    offsets = jax.random.randint(next(keys), (BATCH, 1), 0, POS_OFFSET_MAX, dtype=jnp.int32)
    positions = offsets + jnp.arange(SEQ, dtype=jnp.int32)[None, :]
    return {
        'x': x,
        'p': p,
        'positions': positions,
        'rel_bias_table': small((T5_BUCKETS, DIFF_HEADS), 0.5),
        'attn_norm': gain((DEPTH, D_MODEL)),
        'a_w_qkv': dense((na, D_MODEL, qkv_w), D_MODEL),
        'a_q_norm': gain((na, DIFF_QK_DIM)),
        'a_k_norm': gain((na, DIFF_QK_DIM)),
        'a_lam_q1': small((na, DIFF_QK_DIM), 0.1),
        'a_lam_k1': small((na, DIFF_QK_DIM), 0.1),
        'a_lam_q2': small((na, DIFF_QK_DIM), 0.1),
        'a_lam_k2': small((na, DIFF_QK_DIM), 0.1),
        'a_sub_norm': gain((na, DIFF_V_DIM)),
        'a_w_o': dense((na, DIFF_HEADS * DIFF_V_DIM, D_MODEL), DIFF_HEADS * DIFF_V_DIM),
        'kv_norm': gain((D_MODEL,)),
        'w_dkv': dense((D_MODEL, MLA_KV_LORA + MLA_QK_ROPE), D_MODEL),
        'ckv_norm': gain((MLA_KV_LORA,)),
        'w_ukv': dense((MLA_KV_LORA, MLA_HEADS * (MLA_QK_NOPE + MLA_V_DIM)), MLA_KV_LORA),
        'k_nope_norm': gain((MLA_QK_NOPE,)),
        'k_pe_norm': gain((MLA_QK_ROPE,)),
        'b_w_dq': dense((nbl, D_MODEL, MLA_Q_LORA), D_MODEL),
        'b_cq_norm': gain((nbl, MLA_Q_LORA)),
        'b_w_uq': dense((nbl, MLA_Q_LORA, MLA_HEADS * (MLA_QK_NOPE + MLA_QK_ROPE)), MLA_Q_LORA),
        'b_q_nope_norm': gain((nbl, MLA_QK_NOPE)),
        'b_q_pe_norm': gain((nbl, MLA_QK_ROPE)),
        'b_w_o': dense((nbl, MLA_HEADS * MLA_V_DIM, D_MODEL), MLA_HEADS * MLA_V_DIM),
        'ffn_norm': gain((DEPTH, D_MODEL)),
        'ffn_w_in': dense((DEPTH, D_MODEL, 2 * D_FF), D_MODEL),
        'ffn_conv_w': dense((DEPTH, CONV_WIDTH, 2 * D_FF), CONV_WIDTH),
        'ffn_conv_b': small((DEPTH, 2 * D_FF), 0.02),
        'ffn_w_out': dense((DEPTH, D_FF, D_MODEL), D_FF),
        'ple_norm': gain((DEPTH, D_MODEL)),
        'ple_w_proj': dense((DEPTH, PLE_DIM, D_MODEL), PLE_DIM),
        'ple_w_gate': dense((DEPTH, D_MODEL, D_MODEL), D_MODEL),
    }


def reference(x, p, positions, rel_bias_table, attn_norm,
              a_w_qkv, a_q_norm, a_k_norm, a_lam_q1, a_lam_k1, a_lam_q2, a_lam_k2, a_sub_norm, a_w_o,
              kv_norm, w_dkv, ckv_norm, w_ukv, k_nope_norm, k_pe_norm,
              b_w_dq, b_cq_norm, b_w_uq, b_q_nope_norm, b_q_pe_norm, b_w_o,
              ffn_norm, ffn_w_in, ffn_conv_w, ffn_conv_b, ffn_w_out,
              ple_norm, ple_w_proj, ple_w_gate):
    h = x
    shared = None
    for i in range(DEPTH):
        hn = rms_norm(h, attn_norm[i])
        if i < N_A_LAYERS:
            lam_init = 0.8 - 0.6 * math.exp(-0.3 * i)
            mix = diff_attention(hn, positions, a_w_qkv[i], a_q_norm[i], a_k_norm[i],
                                 a_lam_q1[i], a_lam_k1[i], a_lam_q2[i], a_lam_k2[i],
                                 a_sub_norm[i], a_w_o[i], rel_bias_table, lam_init)
        else:
            if shared is None:
                shared = shared_latent_kv(h, positions, kv_norm, w_dkv, ckv_norm, w_ukv,
                                          k_nope_norm, k_pe_norm)
            j = i - N_A_LAYERS
            k_nope, k_pe, v = shared
            mix = mla_attention(hn, positions, b_w_dq[j], b_cq_norm[j], b_w_uq[j],
                                b_q_nope_norm[j], b_q_pe_norm[j], b_w_o[j], k_nope, k_pe, v)
        h = h + mix
        h = h + conv_gated_ffn(rms_norm(h, ffn_norm[i]), ffn_w_in[i], ffn_conv_w[i],
                               ffn_conv_b[i], ffn_w_out[i])
        h = h + per_layer_embedding(h, p[i], ple_norm[i], ple_w_proj[i], ple_w_gate[i])
    return h
```

```python
import functools
import math

import numpy as np
import jax
import jax.numpy as jnp
from jax import lax
from jax.experimental import pallas as pl
from jax.experimental.pallas import tpu as pltpu

F32 = jnp.float32
BF16 = jnp.bfloat16

DIFF_HEADS = 8
DIFF_QK_DIM = 64
DIFF_V_DIM = 128
T5_BUCKETS = 32
T5_MAX_DISTANCE = 128
MLA_HEADS = 16
MLA_QK_NOPE = 128
MLA_QK_ROPE = 64
MLA_V_DIM = 128
ROPE_BASE = 10000.0
RMS_EPS = 1e-6
NEG_INF = -1e30

LANES = 128
MXU_DIM = 256
VMEM_LIMIT_BYTES = 56 * 1024 * 1024

PROJ_TM = 512
POST_TM = 256
A_T = 256
B_T = 512
FF_CHUNK = 768


def _cparams(sem):
    return pltpu.CompilerParams(dimension_semantics=sem, vmem_limit_bytes=VMEM_LIMIT_BYTES)


def _resident(shape):
    nd = len(shape)
    return pl.BlockSpec(shape, lambda *_: (0,) * nd, pipeline_mode=pl.Buffered(1))


def _rms_rows(x, g):
    ms = jnp.mean(x * x, axis=-1, keepdims=True)
    return x * lax.rsqrt(ms + RMS_EPS) * g


def _group_mean_sq(x, bd, group):
    y = (x * x).astype(BF16)
    parts = [jnp.dot(y[:, c:c + MXU_DIM], bd, preferred_element_type=F32)
             for c in range(0, x.shape[1], MXU_DIM)]
    return jnp.concatenate(parts, axis=1) * (1.0 / group)


def _rope_kernel(pos_ref, cos_ref, sin_ref):
    pos = pos_ref[...].astype(F32)
    lane = lax.broadcasted_iota(jnp.int32, (1, LANES), 1)
    half = MLA_QK_ROPE // 2
    r = (lane % half).astype(F32)
    inv_freq = jnp.exp(-math.log(ROPE_BASE) * r * (2.0 / MLA_QK_ROPE))
    ang = pos * inv_freq
    cos_ref[...] = jnp.cos(ang)
    sin_ref[...] = jnp.where((lane % MLA_QK_ROPE) < half, -jnp.sin(ang), jnp.sin(ang))


def _rope_tables(pos_col):
    t = pos_col.shape[0]
    tm = 2048
    return pl.pallas_call(
        _rope_kernel,
        grid=(t // tm,),
        in_specs=[pl.BlockSpec((tm, 1), lambda i: (i, 0))],
        out_specs=[pl.BlockSpec((tm, LANES), lambda i: (i, 0))] * 2,
        out_shape=[jax.ShapeDtypeStruct((t, LANES), F32)] * 2,
        compiler_params=_cparams(("parallel",)),
        name="rope_tables",
    )(pos_col)


def _a_proj_kernel(h_ref, g_ref, w_ref, gqk_ref, bd_ref, q_ref, k_ref, vt_ref, *, tk):
    tm = h_ref.shape[0]
    qk_w = DIFF_HEADS * 2 * DIFF_QK_DIM
    hn = _rms_rows(h_ref[...], g_ref[...]).astype(BF16)
    qkv = jnp.dot(hn, w_ref[...], preferred_element_type=F32)
    qk = qkv[:, :2 * qk_w]
    ms = _group_mean_sq(qk, bd_ref[...], DIFF_QK_DIM)
    qkn = qk * lax.rsqrt(ms + RMS_EPS) * gqk_ref[...]
    for h in range(DIFF_HEADS):
        q_ref[0, h] = qkn[:, h * LANES:(h + 1) * LANES].astype(BF16)
        k_ref[0, h] = qkn[:, qk_w + h * LANES:qk_w + (h + 1) * LANES].astype(BF16)
        v_h = qkv[:, 2 * qk_w + h * LANES:2 * qk_w + (h + 1) * LANES]
        for c in range(tm // tk):
            vt_ref[0, h, c] = v_h[c * tk:(c + 1) * tk].T.astype(BF16)


def _a_proj(h, g, w_qkv, gqk, bd, *, batch, seq, tk):
    tm = PROJ_TM
    nt = seq // tm
    nh = DIFF_HEADS
    d = h.shape[1]
    return pl.pallas_call(
        functools.partial(_a_proj_kernel, tk=tk),
        grid=(batch * nt,),
        in_specs=[
            pl.BlockSpec((tm, d), lambda i: (i, 0)),
            _resident(g.shape), _resident(w_qkv.shape), _resident(gqk.shape), _resident(bd.shape),
        ],
        out_specs=[
            pl.BlockSpec((1, nh, tm, LANES), lambda i: (i // nt, 0, i % nt, 0)),
            pl.BlockSpec((1, nh, tm, LANES), lambda i: (i // nt, 0, i % nt, 0)),
            pl.BlockSpec((1, nh, tm // tk, DIFF_V_DIM, tk), lambda i: (i // nt, 0, i % nt, 0, 0)),
        ],
        out_shape=[
            jax.ShapeDtypeStruct((batch, nh, seq, LANES), BF16),
            jax.ShapeDtypeStruct((batch, nh, seq, LANES), BF16),
            jax.ShapeDtypeStruct((batch, nh, seq // tk, DIFF_V_DIM, tk), BF16),
        ],
        compiler_params=_cparams(("parallel",)),
        name="a_proj",
    )(h, g, w_qkv, gqk, bd)


def _flash_step(k, qz, vt, extra, m_sc, l_sc, acc_sc):
    s = lax.dot_general(k, qz, (((1,), (1,)), ((), ())), preferred_element_type=F32)
    if extra is not None:
        s = s + extra
    m_old = m_sc[...]
    m_new = jnp.maximum(m_old, jnp.max(s, axis=0, keepdims=True))
    alpha = jnp.exp(m_old - m_new)
    p = jnp.exp(s - m_new)
    l_sc[...] = alpha * l_sc[...] + jnp.sum(p, axis=0, keepdims=True)
    acc_sc[...] = alpha * acc_sc[...] + jnp.dot(vt, p.astype(BF16), preferred_element_type=F32)
    m_sc[...] = m_new


def _flash_init(m_sc, l_sc, acc_sc):
    m_sc[...] = jnp.full(m_sc.shape, NEG_INF, F32)
    l_sc[...] = jnp.zeros(l_sc.shape, F32)
    acc_sc[...] = jnp.zeros(acc_sc.shape, F32)


def _a_attn_kernel(q_ref, k_ref, vt_ref, bias_ref, lam_ref, sg_ref, o_ref, m_sc, l_sc, acc_sc,
                   *, lam_init, t):
    qi = pl.program_id(2)
    q = q_ref[0, 0]
    lane = lax.broadcasted_iota(jnp.int32, q.shape, 1)
    zero = jnp.zeros_like(q)
    qz = jnp.concatenate([jnp.where(lane < DIFF_QK_DIM, q, zero),
                          jnp.where(lane >= DIFF_QK_DIM, q, zero)], axis=0)
    _flash_init(m_sc, l_sc, acc_sc)

    def step(kj, extra):
        k = k_ref[0, 0, pl.ds(pl.multiple_of(kj * t, t), t), :]
        _flash_step(k, qz, vt_ref[0, 0, kj], extra, m_sc, l_sc, acc_sc)

    def far_body(kj, carry):
        step(kj, None)
        return carry

    lax.fori_loop(0, jnp.maximum(qi - 1, 0), far_body, 0)

    @pl.when(qi >= 1)
    def _():
        step(qi - 1, bias_ref[0, 1])

    step(qi, bias_ref[0, 0])

    lp = lam_ref[...]
    lam = (jnp.exp(jnp.sum(lp[0:1] * lp[1:2], keepdims=True))
           - jnp.exp(jnp.sum(lp[2:3] * lp[3:4], keepdims=True)) + lam_init)
    acc = acc_sc[...]
    inv = 1.0 / l_sc[...]
    o_t = acc[:, :t] * inv[:, :t] - lam * (acc[:, t:] * inv[:, t:])
    o = _rms_rows(o_t.T, sg_ref[...]) * (1.0 - lam_init)
    o_ref[0] = o.astype(BF16)


def _a_attn(q, k, vt, bias, lam_params, sub_gain, *, lam_init):
    batch, nh, seq, _ = q.shape
    t = A_T
    nq = seq // t
    return pl.pallas_call(
        functools.partial(_a_attn_kernel, lam_init=lam_init, t=t),
        grid=(batch, nh, nq),
        in_specs=[
            pl.BlockSpec((1, 1, t, LANES), lambda b, h, i: (b, h, i, 0)),
            pl.BlockSpec((1, 1, seq, LANES), lambda b, h, i: (b, h, 0, 0)),
            pl.BlockSpec((1, 1, nq, DIFF_V_DIM, t), lambda b, h, i: (b, h, 0, 0, 0)),
            pl.BlockSpec((1, 2, t, 2 * t), lambda b, h, i: (h, 0, 0, 0)),
            pl.BlockSpec(lam_params.shape, lambda b, h, i: (0, 0)),
            pl.BlockSpec(sub_gain.shape, lambda b, h, i: (0, 0)),
        ],
        out_specs=pl.BlockSpec((1, t, DIFF_V_DIM), lambda b, h, i: (b, i, h)),
        out_shape=jax.ShapeDtypeStruct((batch, seq, nh * DIFF_V_DIM), BF16),
        scratch_shapes=[
            pltpu.VMEM((1, 2 * t), F32), pltpu.VMEM((1, 2 * t), F32),
            pltpu.VMEM((DIFF_V_DIM, 2 * t), F32),
        ],
        compiler_params=_cparams(("parallel", "parallel", "arbitrary")),
        name="a_attn",
    )(q, k, vt, bias, lam_params, sub_gain)


def _kv_proj_kernel(h_ref, g_ref, wd_ref, cg_ref, pg_ref, pgs_ref, cos_ref, sin_ref, wu_ref, kg_ref,
                    k_ref, vt_ref, *, tk):
    tm = h_ref.shape[0]
    lora = cg_ref.shape[1]
    hn = _rms_rows(h_ref[...], g_ref[...]).astype(BF16)
    ckv = jnp.dot(hn, wd_ref[...], preferred_element_type=F32)
    c_kv = _rms_rows(ckv[:, :lora], cg_ref[...]).astype(BF16)
    pe = ckv[:, lora:lora + LANES]
    pe_sw = ckv[:, lora + LANES:]
    r = lax.rsqrt(jnp.mean(pe * pe, axis=-1, keepdims=True) + RMS_EPS)
    k_pe = ((pe * r * pg_ref[...]) * cos_ref[...] + (pe_sw * r * pgs_ref[...]) * sin_ref[...])
    lane = lax.broadcasted_iota(jnp.int32, k_pe.shape, 1)
    zero = jnp.zeros_like(k_pe)
    pe_even = jnp.where(lane < MLA_QK_ROPE, k_pe, zero).astype(BF16)
    pe_odd = jnp.where(lane >= MLA_QK_ROPE, k_pe, zero).astype(BF16)
    kv = jnp.dot(c_kv, wu_ref[...], preferred_element_type=F32)
    nope_w = MLA_HEADS * MLA_QK_NOPE
    for h in range(MLA_HEADS):
        kn = _rms_rows(kv[:, h * LANES:(h + 1) * LANES], kg_ref[...])
        k_ref[0, h, :, :LANES] = kn.astype(BF16)
        k_ref[0, h, :, LANES:] = pe_even if h % 2 == 0 else pe_odd
        v_h = kv[:, nope_w + h * LANES:nope_w + (h + 1) * LANES]
        for c in range(tm // tk):
            vt_ref[0, h, c] = v_h[c * tk:(c + 1) * tk].T.astype(BF16)


def _kv_proj(h, g, wd, cg, pg, pgs, cos_t, sin_t, wu, kg, *, batch, seq, tk):
    tm = PROJ_TM
    nt = seq // tm
    nh = MLA_HEADS
    d = h.shape[1]
    return pl.pallas_call(
        functools.partial(_kv_proj_kernel, tk=tk),
        grid=(batch * nt,),
        in_specs=[
            pl.BlockSpec((tm, d), lambda i: (i, 0)),
            _resident(g.shape), _resident(wd.shape), _resident(cg.shape),
            _resident(pg.shape), _resident(pgs.shape),
            pl.BlockSpec((tm, LANES), lambda i: (i, 0)),
            pl.BlockSpec((tm, LANES), lambda i: (i, 0)),
            _resident(wu.shape), _resident(kg.shape),
        ],
        out_specs=[
            pl.BlockSpec((1, nh, tm, 2 * LANES), lambda i: (i // nt, 0, i % nt, 0)),
            pl.BlockSpec((1, nh, tm // tk, MLA_V_DIM, tk), lambda i: (i // nt, 0, i % nt, 0, 0)),
        ],
        out_shape=[
            jax.ShapeDtypeStruct((batch, nh, seq, 2 * LANES), BF16),
            jax.ShapeDtypeStruct((batch, nh, seq // tk, MLA_V_DIM, tk), BF16),
        ],
        compiler_params=_cparams(("parallel",)),
        name="kv_proj",
    )(h, g, wd, cg, pg, pgs, cos_t, sin_t, wu, kg)


def _b_proj_kernel(h_ref, g_ref, wdq_ref, cqg_ref, wuq_ref, gn_ref, gp_ref, gps_ref, cos_ref, sin_ref,
                   bd_ref, q_ref):
    nope_w = MLA_HEADS * MLA_QK_NOPE
    pe_w = MLA_HEADS * MLA_QK_ROPE
    hn = _rms_rows(h_ref[...], g_ref[...]).astype(BF16)
    cq = jnp.dot(hn, wdq_ref[...], preferred_element_type=F32)
    cqn = _rms_rows(cq, cqg_ref[...]).astype(BF16)
    q = jnp.dot(cqn, wuq_ref[...], preferred_element_type=F32)
    qp = q[:, nope_w:nope_w + pe_w]
    qps = q[:, nope_w + pe_w:]
    r = lax.rsqrt(_group_mean_sq(qp, bd_ref[...], MLA_QK_ROPE) + RMS_EPS)
    reps = pe_w // LANES
    cos_t = jnp.concatenate([cos_ref[...]] * reps, axis=1)
    sin_t = jnp.concatenate([sin_ref[...]] * reps, axis=1)
    rot = (qp * r * gp_ref[...]) * cos_t + (qps * r * gps_ref[...]) * sin_t
    lane = lax.broadcasted_iota(jnp.int32, (q.shape[0], LANES), 1)
    zero = jnp.zeros((q.shape[0], LANES), F32)
    for h in range(MLA_HEADS):
        qn = _rms_rows(q[:, h * LANES:(h + 1) * LANES], gn_ref[...])
        pair = rot[:, (h // 2) * LANES:(h // 2 + 1) * LANES]
        keep = (lane < MLA_QK_ROPE) if h % 2 == 0 else (lane >= MLA_QK_ROPE)
        q_ref[:, h * 2 * LANES:h * 2 * LANES + LANES] = qn.astype(BF16)
        q_ref[:, h * 2 * LANES + LANES:(h + 1) * 2 * LANES] = jnp.where(keep, pair, zero).astype(BF16)


def _b_proj(h, g, wdq, cqg, wuq, gn, gp, gps, cos_t, sin_t, bd):
    t, d = h.shape
    tm = PROJ_TM
    qw = MLA_HEADS * 2 * LANES
    return pl.pallas_call(
        _b_proj_kernel,
        grid=(t // tm,),
        in_specs=[
            pl.BlockSpec((tm, d), lambda i: (i, 0)),
            _resident(g.shape), _resident(wdq.shape), _resident(cqg.shape), _resident(wuq.shape),
            _resident(gn.shape), _resident(gp.shape), _resident(gps.shape),
            pl.BlockSpec((tm, LANES), lambda i: (i, 0)),
            pl.BlockSpec((tm, LANES), lambda i: (i, 0)),
            _resident(bd.shape),
        ],
        out_specs=pl.BlockSpec((tm, qw), lambda i: (i, 0)),
        out_shape=jax.ShapeDtypeStruct((t, qw), BF16),
        compiler_params=_cparams(("parallel",)),
        name="b_proj",
    )(h, g, wdq, cqg, wuq, gn, gp, gps, cos_t, sin_t, bd)


def _b_attn_kernel(q_ref, k_ref, vt_ref, mask_ref, o_ref, m_sc, l_sc, acc_sc, *, t):
    qi = pl.program_id(2)
    q = q_ref[...]
    _flash_init(m_sc, l_sc, acc_sc)

    def step(kj, extra):
        k = k_ref[0, 0, pl.ds(pl.multiple_of(kj * t, t), t), :]
        _flash_step(k, q, vt_ref[0, 0, kj], extra, m_sc, l_sc, acc_sc)

    def body(kj, carry):
        step(kj, None)
        return carry

    lax.fori_loop(0, qi, body, 0)
    step(qi, mask_ref[...])
    o_t = acc_sc[...] * (1.0 / l_sc[...])
    o_ref[...] = o_t.T.astype(BF16)


def _b_attn(q, k, vt, mask, *, batch, seq):
    nh = MLA_HEADS
    t = B_T
    nq = seq // t
    return pl.pallas_call(
        functools.partial(_b_attn_kernel, t=t),
        grid=(batch, nh, nq),
        in_specs=[
            pl.BlockSpec((t, 2 * LANES), lambda b, h, i: (b * nq + i, h)),
            pl.BlockSpec((1, 1, seq, 2 * LANES), lambda b, h, i: (b, h, 0, 0)),
            pl.BlockSpec((1, 1, nq, MLA_V_DIM, t), lambda b, h, i: (b, h, 0, 0, 0)),
            pl.BlockSpec(mask.shape, lambda b, h, i: (0, 0)),
        ],
        out_specs=pl.BlockSpec((t, MLA_V_DIM), lambda b, h, i: (b * nq + i, h)),
        out_shape=jax.ShapeDtypeStruct((batch * seq, nh * MLA_V_DIM), BF16),
        scratch_shapes=[
            pltpu.VMEM((1, t), F32), pltpu.VMEM((1, t), F32), pltpu.VMEM((MLA_V_DIM, t), F32),
        ],
        compiler_params=_cparams(("parallel", "parallel", "arbitrary")),
        name="b_attn",
    )(q, k, vt, mask)


def _post_kernel(h_ref, o_ref, wo_ref, fg_ref, win_ref, cw_ref, cb_ref, wout_ref, pg_ref, wg_ref,
                 p_ref, wp_ref, out_ref, carry_sc, *, tiles_per_batch, d_ff):
    tm = h_ref.shape[0]
    h1 = h_ref[...] + jnp.dot(o_ref[...], wo_ref[...], preferred_element_type=F32)
    hn = _rms_rows(h1, fg_ref[...]).astype(BF16)

    @pl.when((pl.program_id(0) % tiles_per_batch) == 0)
    def _():
        carry_sc[...] = jnp.zeros(carry_sc.shape, F32)

    def conv(lo, width):
        u = jnp.dot(hn, win_ref[:, lo:lo + width], preferred_element_type=F32)
        prev = carry_sc[:, lo:lo + width]
        carry_sc[:, lo:lo + width] = u[tm - 8:, :]
        ext = jnp.concatenate([prev, u], axis=0)
        cw = cw_ref[:, lo:lo + width]
        return (cb_ref[:, lo:lo + width] + ext[6:6 + tm] * cw[0:1] + ext[7:7 + tm] * cw[1:2]
                + u * cw[2:3])

    acc = jnp.zeros((tm, out_ref.shape[1]), F32)
    lo = 0
    while lo < d_ff:
        width = min(FF_CHUNK, d_ff - lo)
        a = conv(lo, width)
        gt = conv(d_ff + lo, width)
        act = (gt * jax.nn.sigmoid(gt) * a).astype(BF16)
        acc = acc + jnp.dot(act, wout_ref[lo:lo + width, :], preferred_element_type=F32)
        lo += width
    h2 = h1 + acc
    gate = jax.nn.sigmoid(jnp.dot(_rms_rows(h2, pg_ref[...]).astype(BF16), wg_ref[...],
                                  preferred_element_type=F32))
    pe = jnp.dot(p_ref[...].astype(BF16), wp_ref[...], preferred_element_type=F32)
    out_ref[...] = h2 + pe * gate


def _post(h, o, wo, fg, win, cw, cb, wout, pg, wg, p_i, wp, *, seq):
    t, d = h.shape
    tm = POST_TM
    d_ff = wout.shape[0]
    return pl.pallas_call(
        functools.partial(_post_kernel, tiles_per_batch=seq // tm, d_ff=d_ff),
        grid=(t // tm,),
        in_specs=[
            pl.BlockSpec((tm, d), lambda i: (i, 0)),
            pl.BlockSpec((tm, o.shape[1]), lambda i: (i, 0)),
            _resident(wo.shape), _resident(fg.shape), _resident(win.shape), _resident(cw.shape),
            _resident(cb.shape), _resident(wout.shape), _resident(pg.shape), _resident(wg.shape),
            pl.BlockSpec((tm, p_i.shape[1]), lambda i: (i, 0)),
            _resident(wp.shape),
        ],
        out_specs=pl.BlockSpec((tm, d), lambda i: (i, 0)),
        out_shape=jax.ShapeDtypeStruct((t, d), F32),
        scratch_shapes=[pltpu.VMEM((8, 2 * d_ff), F32)],
        compiler_params=_cparams(("arbitrary",)),
        name="post",
    )(h, o, wo, fg, win, cw, cb, wout, pg, wg, p_i, wp)


def _t5_bucket_table():
    n = np.arange(T5_MAX_DISTANCE + 1)
    max_exact = T5_BUCKETS // 2
    log_ratio = np.log(np.maximum(n, 1) / max_exact) / math.log(T5_MAX_DISTANCE / max_exact)
    large = np.minimum(max_exact + (log_ratio * (T5_BUCKETS - max_exact)).astype(np.int32),
                       T5_BUCKETS - 1)
    return np.where(n < max_exact, n, large).astype(np.int32)


def _bias_tiles(rel_table, t):
    rel = rel_table[_t5_bucket_table()]
    rel = rel - rel[T5_MAX_DISTANCE]
    kk = np.arange(t)[:, None]
    qq = np.arange(t)[None, :]
    d0 = qq - kk
    d1 = t + qq - kk
    b0 = jnp.where((d0 >= 0)[..., None], rel[np.clip(d0, 0, T5_MAX_DISTANCE)], NEG_INF)
    b1 = rel[np.clip(d1, 0, T5_MAX_DISTANCE)]
    tiles = jnp.stack([b0, b1], axis=0).transpose(3, 0, 1, 2)
    return jnp.concatenate([tiles, tiles], axis=-1).astype(F32)


def _causal_tile(t):
    kk = np.arange(t)[:, None]
    qq = np.arange(t)[None, :]
    return jnp.asarray(np.where(qq >= kk, 0.0, NEG_INF), F32)


def _block_diag_ones(group):
    idx = np.arange(MXU_DIM) // group
    return jnp.asarray(idx[:, None] == idx[None, :], BF16)


def _swap_halves(n_groups, width):
    base = np.arange(n_groups * width).reshape(n_groups, 2, width // 2)
    return base[:, ::-1, :].reshape(-1)


def kernel(x, p, positions, rel_bias_table, attn_norm, a_w_qkv, a_q_norm, a_k_norm, a_lam_q1, a_lam_k1, a_lam_q2, a_lam_k2, a_sub_norm, a_w_o, kv_norm, w_dkv, ckv_norm, w_ukv, k_nope_norm, k_pe_norm, b_w_dq, b_cq_norm, b_w_uq, b_q_nope_norm, b_q_pe_norm, b_w_o, ffn_norm, ffn_w_in, ffn_conv_w, ffn_conv_b, ffn_w_out, ple_norm, ple_w_proj, ple_w_gate):
    batch, seq, d = x.shape
    depth = p.shape[0]
    n_a = a_w_qkv.shape[0]
    t = batch * seq
    row = lambda v: v.reshape(1, -1).astype(F32)

    h = x.reshape(t, d)
    bd64 = _block_diag_ones(DIFF_QK_DIM)
    bias = _bias_tiles(rel_bias_table, A_T)
    cos_t, sin_t = _rope_tables(positions.reshape(t, 1))
    shared = None

    for i in range(depth):
        if i < n_a:
            lam_init = 0.8 - 0.6 * math.exp(-0.3 * i)
            gqk = jnp.concatenate([jnp.tile(a_q_norm[i], 2 * DIFF_HEADS) * DIFF_QK_DIM ** -0.5,
                                   jnp.tile(a_k_norm[i], 2 * DIFF_HEADS)]).reshape(1, -1)
            q, k, vt = _a_proj(h, row(attn_norm[i]), a_w_qkv[i].astype(BF16), gqk, bd64,
                               batch=batch, seq=seq, tk=A_T)
            lam_params = jnp.stack([a_lam_q1[i], a_lam_k1[i], a_lam_q2[i], a_lam_k2[i]])
            o = _a_attn(q, k, vt, bias, lam_params, row(a_sub_norm[i]), lam_init=lam_init)
            o = o.reshape(t, -1)
            w_o = a_w_o[i]
        else:
            j = i - n_a
            if shared is None:
                lora = ckv_norm.shape[0]
                sw = _swap_halves(1, MLA_QK_ROPE)
                w_pe = w_dkv[:, lora:]
                wd = jnp.concatenate([w_dkv[:, :lora], w_pe, w_pe, w_pe[:, sw], w_pe[:, sw]], axis=1)
                hw = MLA_QK_NOPE + MLA_V_DIM
                cols = np.arange(MLA_HEADS * hw).reshape(MLA_HEADS, hw)
                perm = np.concatenate([cols[:, :MLA_QK_NOPE].reshape(-1), cols[:, MLA_QK_NOPE:].reshape(-1)])
                shared = _kv_proj(h, row(kv_norm), wd.astype(BF16), row(ckv_norm),
                                  row(jnp.tile(k_pe_norm, 2)), row(jnp.tile(k_pe_norm[sw], 2)),
                                  cos_t, sin_t, w_ukv[:, perm].astype(BF16), row(k_nope_norm),
                                  batch=batch, seq=seq, tk=B_T)
                mask = _causal_tile(B_T)
            hw = MLA_QK_NOPE + MLA_QK_ROPE
            cols = np.arange(MLA_HEADS * hw).reshape(MLA_HEADS, hw)
            nope_cols = cols[:, :MLA_QK_NOPE].reshape(-1)
            pe_cols = cols[:, MLA_QK_NOPE:].reshape(-1)
            sw = _swap_halves(MLA_HEADS, MLA_QK_ROPE)
            wuq = b_w_uq[j][:, np.concatenate([nope_cols, pe_cols, pe_cols[sw]])]
            scale = hw ** -0.5
            gp = jnp.tile(b_q_pe_norm[j], MLA_HEADS)
            qf = _b_proj(h, row(attn_norm[i]), b_w_dq[j].astype(BF16), row(b_cq_norm[j]),
                         wuq.astype(BF16), row(b_q_nope_norm[j] * scale), row(gp * scale),
                         row(gp[sw] * scale), cos_t, sin_t, bd64)
            o = _b_attn(qf, shared[0], shared[1], mask, batch=batch, seq=seq)
            w_o = b_w_o[j]
        h = _post(h, o, w_o.astype(BF16), row(ffn_norm[i]), ffn_w_in[i].astype(BF16),
                  ffn_conv_w[i], row(ffn_conv_b[i]), ffn_w_out[i].astype(BF16), row(ple_norm[i]),
                  ple_w_gate[i].astype(BF16), p[i].reshape(t, -1), ple_w_proj[i].astype(BF16),
                  seq=seq)
    return h.reshape(batch, seq, d)
```

```python
import functools
import math

import numpy as np
import jax
import jax.numpy as jnp
from jax import lax
from jax.experimental import pallas as pl
from jax.experimental.pallas import tpu as pltpu

F32 = jnp.float32
BF16 = jnp.bfloat16

DIFF_HEADS = 8
DIFF_QK_DIM = 64
DIFF_V_DIM = 128
T5_BUCKETS = 32
T5_MAX_DISTANCE = 128
MLA_HEADS = 16
MLA_QK_NOPE = 128
MLA_QK_ROPE = 64
MLA_V_DIM = 128
ROPE_BASE = 10000.0
RMS_EPS = 1e-6
NEG_INF = -1e30
LOG2E = math.log2(math.e)

LANES = 128
MXU_DIM = 256
VMEM_LIMIT_BYTES = 56 * 1024 * 1024

PROJ_TM = 512
POST_TM = 256
A_T = 256
A_G = 2
B_T = 512
B_G = 2
FF_CHUNK = 768


def _cparams(sem):
    return pltpu.CompilerParams(dimension_semantics=sem, vmem_limit_bytes=VMEM_LIMIT_BYTES)


def _resident(shape):
    nd = len(shape)
    return pl.BlockSpec(shape, lambda *_: (0,) * nd, pipeline_mode=pl.Buffered(1))


def _rms_rows(x, g):
    ms = jnp.mean(x * x, axis=-1, keepdims=True)
    return x * lax.rsqrt(ms + RMS_EPS) * g


def _group_mean_sq(x, bd, group):
    y = (x * x).astype(BF16)
    parts = [jnp.dot(y[:, c:c + MXU_DIM], bd, preferred_element_type=F32)
             for c in range(0, x.shape[1], MXU_DIM)]
    return jnp.concatenate(parts, axis=1) * (1.0 / group)


def _rope_kernel(pos_ref, cos_ref, sin_ref):
    pos = pos_ref[...].astype(F32)
    lane = lax.broadcasted_iota(jnp.int32, (1, LANES), 1)
    half = MLA_QK_ROPE // 2
    r = (lane % half).astype(F32)
    inv_freq = jnp.exp(-math.log(ROPE_BASE) * r * (2.0 / MLA_QK_ROPE))
    ang = pos * inv_freq
    cos_ref[...] = jnp.cos(ang)
    sin_ref[...] = jnp.where((lane % MLA_QK_ROPE) < half, -jnp.sin(ang), jnp.sin(ang))


def _rope_tables(pos_col):
    t = pos_col.shape[0]
    tm = 2048
    return pl.pallas_call(
        _rope_kernel,
        grid=(t // tm,),
        in_specs=[pl.BlockSpec((tm, 1), lambda i: (i, 0))],
        out_specs=[pl.BlockSpec((tm, LANES), lambda i: (i, 0))] * 2,
        out_shape=[jax.ShapeDtypeStruct((t, LANES), F32)] * 2,
        compiler_params=_cparams(("parallel",)),
        name="rope_tables",
    )(pos_col)


def _a_proj_kernel(h_ref, g_ref, w_ref, gqk_ref, bd_ref, q_ref, k_ref, vt_ref, *, tk):
    tm = h_ref.shape[0]
    qk_w = DIFF_HEADS * 2 * DIFF_QK_DIM
    hn = _rms_rows(h_ref[...], g_ref[...]).astype(BF16)
    qkv = jnp.dot(hn, w_ref[...], preferred_element_type=F32)
    qk = qkv[:, :2 * qk_w]
    ms = _group_mean_sq(qk, bd_ref[...], DIFF_QK_DIM)
    qkn = qk * lax.rsqrt(ms + RMS_EPS) * gqk_ref[...]
    for h in range(DIFF_HEADS):
        q_ref[0, h] = qkn[:, h * LANES:(h + 1) * LANES].astype(BF16)
        k_ref[0, h] = qkn[:, qk_w + h * LANES:qk_w + (h + 1) * LANES].astype(BF16)
        v_h = qkv[:, 2 * qk_w + h * LANES:2 * qk_w + (h + 1) * LANES]
        for c in range(tm // tk):
            vt_ref[0, h, c] = v_h[c * tk:(c + 1) * tk].T.astype(BF16)


def _a_proj(h, g, w_qkv, gqk, bd, *, batch, seq, tk):
    tm = PROJ_TM
    nt = seq // tm
    nh = DIFF_HEADS
    d = h.shape[1]
    return pl.pallas_call(
        functools.partial(_a_proj_kernel, tk=tk),
        grid=(batch * nt,),
        in_specs=[
            pl.BlockSpec((tm, d), lambda i: (i, 0)),
            _resident(g.shape), _resident(w_qkv.shape), _resident(gqk.shape), _resident(bd.shape),
        ],
        out_specs=[
            pl.BlockSpec((1, nh, tm, LANES), lambda i: (i // nt, 0, i % nt, 0)),
            pl.BlockSpec((1, nh, tm, LANES), lambda i: (i // nt, 0, i % nt, 0)),
            pl.BlockSpec((1, nh, tm // tk, DIFF_V_DIM, tk), lambda i: (i // nt, 0, i % nt, 0, 0)),
        ],
        out_shape=[
            jax.ShapeDtypeStruct((batch, nh, seq, LANES), BF16),
            jax.ShapeDtypeStruct((batch, nh, seq, LANES), BF16),
            jax.ShapeDtypeStruct((batch, nh, seq // tk, DIFF_V_DIM, tk), BF16),
        ],
        compiler_params=_cparams(("parallel",)),
        name="a_proj",
    )(h, g, w_qkv, gqk, bd)


def _scores(k, qz, s_ref):
    s_ref[...] = lax.dot_general(k, qz, (((1,), (1,)), ((), ())), preferred_element_type=F32)


def _softmax_pv(s_ref, extra, vt, m_sc, l_sc, acc_sc):
    s = s_ref[...]
    if extra is not None:
        s = s + extra
    m_old = m_sc[...]
    m_new = jnp.maximum(m_old, jnp.max(s, axis=0, keepdims=True))
    alpha = jnp.exp2(m_old - m_new)
    p = jnp.exp2(s - m_new)
    l_sc[...] = alpha * l_sc[...] + jnp.sum(p, axis=0, keepdims=True)
    acc_sc[...] = alpha * acc_sc[...] + jnp.dot(vt, p.astype(BF16), preferred_element_type=F32)
    m_sc[...] = m_new


def _flash_init(m_sc, l_sc, acc_sc):
    m_sc[...] = jnp.full(m_sc.shape, NEG_INF, F32)
    l_sc[...] = jnp.zeros(l_sc.shape, F32)
    acc_sc[...] = jnp.zeros(acc_sc.shape, F32)


def _causal_pipeline(n_far, qk, sm, tails):
    qk(0, 0)

    def body(jj, carry):
        kj = 2 * jj
        qk(1, kj + 1)
        sm(0, kj, None)
        qk(0, kj + 2)
        sm(1, kj + 1, None)
        return carry

    lax.fori_loop(0, n_far // 2, body, 0)

    for cond, blocks in tails:
        @pl.when(cond)
        def _(blocks=blocks):
            for n, (kj, kind) in enumerate(blocks):
                if n + 1 < len(blocks):
                    qk(1 - n % 2, blocks[n + 1][0])
                sm(n % 2, kj, kind)


def _a_attn_kernel(q_ref, k_ref, vt_ref, bias_ref, lam_ref, sg_ref, o_ref, m_sc, l_sc, acc_sc, s_sc,
                   *, lam_init, t, g):
    qi = pl.program_id(2)
    qz = []
    for gi in range(g):
        q = q_ref[0, gi]
        lane = lax.broadcasted_iota(jnp.int32, q.shape, 1)
        zero = jnp.zeros_like(q)
        qz.append(jnp.concatenate([jnp.where(lane < DIFF_QK_DIM, q, zero),
                                   jnp.where(lane >= DIFF_QK_DIM, q, zero)], axis=0))
    _flash_init(m_sc, l_sc, acc_sc)

    def qk(slot, kj):
        for gi in range(g):
            k = k_ref[0, gi, pl.ds(pl.multiple_of(kj * t, t), t), :]
            _scores(k, qz[gi], s_sc.at[slot, gi])

    def sm(slot, kj, kind):
        for gi in range(g):
            extra = None if kind is None else bias_ref[gi, kind]
            _softmax_pv(s_sc.at[slot, gi], extra, vt_ref[0, gi, kj],
                        m_sc.at[gi], l_sc.at[gi], acc_sc.at[gi])

    n_far = jnp.maximum(qi - 1, 0)
    odd = n_far % 2 == 1
    _causal_pipeline(n_far, qk, sm, [
        (qi == 0, [(qi, 0)]),
        ((qi >= 1) & jnp.logical_not(odd), [(qi - 1, 1), (qi, 0)]),
        (odd, [(qi - 2, None), (qi - 1, 1), (qi, 0)]),
    ])

    lp = lam_ref[...]
    lam = (jnp.exp(jnp.sum(lp[0:1] * lp[1:2], keepdims=True))
           - jnp.exp(jnp.sum(lp[2:3] * lp[3:4], keepdims=True)) + lam_init)
    for gi in range(g):
        acc = acc_sc[gi]
        inv = 1.0 / l_sc[gi]
        o_t = acc[:, :t] * inv[:, :t] - lam * (acc[:, t:] * inv[:, t:])
        o = _rms_rows(o_t.T, sg_ref[...]) * (1.0 - lam_init)
        o_ref[0, :, gi * DIFF_V_DIM:(gi + 1) * DIFF_V_DIM] = o.astype(BF16)


def _a_attn(q, k, vt, bias, lam_params, sub_gain, *, lam_init):
    batch, nh, seq, _ = q.shape
    t = A_T
    g = A_G
    nq = seq // t
    return pl.pallas_call(
        functools.partial(_a_attn_kernel, lam_init=lam_init, t=t, g=g),
        grid=(batch, nh // g, nq),
        in_specs=[
            pl.BlockSpec((1, g, t, LANES), lambda b, h, i: (b, h, i, 0)),
            pl.BlockSpec((1, g, seq, LANES), lambda b, h, i: (b, h, 0, 0)),
            pl.BlockSpec((1, g, nq, DIFF_V_DIM, t), lambda b, h, i: (b, h, 0, 0, 0)),
            pl.BlockSpec((g, 2, t, 2 * t), lambda b, h, i: (h, 0, 0, 0)),
            pl.BlockSpec(lam_params.shape, lambda b, h, i: (0, 0)),
            pl.BlockSpec(sub_gain.shape, lambda b, h, i: (0, 0)),
        ],
        out_specs=pl.BlockSpec((1, t, g * DIFF_V_DIM), lambda b, h, i: (b, i, h)),
        out_shape=jax.ShapeDtypeStruct((batch, seq, nh * DIFF_V_DIM), BF16),
        scratch_shapes=[
            pltpu.VMEM((g, 1, 2 * t), F32), pltpu.VMEM((g, 1, 2 * t), F32),
            pltpu.VMEM((g, DIFF_V_DIM, 2 * t), F32), pltpu.VMEM((2, g, t, 2 * t), F32),
        ],
        compiler_params=_cparams(("parallel", "parallel", "arbitrary")),
        name="a_attn",
    )(q, k, vt, bias, lam_params, sub_gain)


def _kv_proj_kernel(h_ref, g_ref, wd_ref, cg_ref, pg_ref, pgs_ref, cos_ref, sin_ref, wu_ref, kg_ref,
                    k_ref, vt_ref, *, tk):
    tm = h_ref.shape[0]
    lora = cg_ref.shape[1]
    hn = _rms_rows(h_ref[...], g_ref[...]).astype(BF16)
    ckv = jnp.dot(hn, wd_ref[...], preferred_element_type=F32)
    c_kv = _rms_rows(ckv[:, :lora], cg_ref[...]).astype(BF16)
    pe = ckv[:, lora:lora + LANES]
    pe_sw = ckv[:, lora + LANES:]
    r = lax.rsqrt(jnp.mean(pe * pe, axis=-1, keepdims=True) + RMS_EPS)
    k_pe = ((pe * r * pg_ref[...]) * cos_ref[...] + (pe_sw * r * pgs_ref[...]) * sin_ref[...])
    lane = lax.broadcasted_iota(jnp.int32, k_pe.shape, 1)
    zero = jnp.zeros_like(k_pe)
    pe_even = jnp.where(lane < MLA_QK_ROPE, k_pe, zero).astype(BF16)
    pe_odd = jnp.where(lane >= MLA_QK_ROPE, k_pe, zero).astype(BF16)
    kv = jnp.dot(c_kv, wu_ref[...], preferred_element_type=F32)
    nope_w = MLA_HEADS * MLA_QK_NOPE
    for h in range(MLA_HEADS):
        kn = _rms_rows(kv[:, h * LANES:(h + 1) * LANES], kg_ref[...])
        k_ref[0, h, :, :LANES] = kn.astype(BF16)
        k_ref[0, h, :, LANES:] = pe_even if h % 2 == 0 else pe_odd
        v_h = kv[:, nope_w + h * LANES:nope_w + (h + 1) * LANES]
        for c in range(tm // tk):
            vt_ref[0, h, c] = v_h[c * tk:(c + 1) * tk].T.astype(BF16)


def _kv_proj(h, g, wd, cg, pg, pgs, cos_t, sin_t, wu, kg, *, batch, seq, tk):
    tm = PROJ_TM
    nt = seq // tm
    nh = MLA_HEADS
    d = h.shape[1]
    return pl.pallas_call(
        functools.partial(_kv_proj_kernel, tk=tk),
        grid=(batch * nt,),
        in_specs=[
            pl.BlockSpec((tm, d), lambda i: (i, 0)),
            _resident(g.shape), _resident(wd.shape), _resident(cg.shape),
            _resident(pg.shape), _resident(pgs.shape),
            pl.BlockSpec((tm, LANES), lambda i: (i, 0)),
            pl.BlockSpec((tm, LANES), lambda i: (i, 0)),
            _resident(wu.shape), _resident(kg.shape),
        ],
        out_specs=[
            pl.BlockSpec((1, nh, tm, 2 * LANES), lambda i: (i // nt, 0, i % nt, 0)),
            pl.BlockSpec((1, nh, tm // tk, MLA_V_DIM, tk), lambda i: (i // nt, 0, i % nt, 0, 0)),
        ],
        out_shape=[
            jax.ShapeDtypeStruct((batch, nh, seq, 2 * LANES), BF16),
            jax.ShapeDtypeStruct((batch, nh, seq // tk, MLA_V_DIM, tk), BF16),
        ],
        compiler_params=_cparams(("parallel",)),
        name="kv_proj",
    )(h, g, wd, cg, pg, pgs, cos_t, sin_t, wu, kg)


def _b_proj_kernel(h_ref, g_ref, wdq_ref, cqg_ref, wuq_ref, gn_ref, gp_ref, gps_ref, cos_ref, sin_ref,
                   bd_ref, q_ref):
    nope_w = MLA_HEADS * MLA_QK_NOPE
    pe_w = MLA_HEADS * MLA_QK_ROPE
    hn = _rms_rows(h_ref[...], g_ref[...]).astype(BF16)
    cq = jnp.dot(hn, wdq_ref[...], preferred_element_type=F32)
    cqn = _rms_rows(cq, cqg_ref[...]).astype(BF16)
    q = jnp.dot(cqn, wuq_ref[...], preferred_element_type=F32)
    qp = q[:, nope_w:nope_w + pe_w]
    qps = q[:, nope_w + pe_w:]
    r = lax.rsqrt(_group_mean_sq(qp, bd_ref[...], MLA_QK_ROPE) + RMS_EPS)
    reps = pe_w // LANES
    cos_t = jnp.concatenate([cos_ref[...]] * reps, axis=1)
    sin_t = jnp.concatenate([sin_ref[...]] * reps, axis=1)
    rot = (qp * r * gp_ref[...]) * cos_t + (qps * r * gps_ref[...]) * sin_t
    lane = lax.broadcasted_iota(jnp.int32, (q.shape[0], LANES), 1)
    zero = jnp.zeros((q.shape[0], LANES), F32)
    for h in range(MLA_HEADS):
        qn = _rms_rows(q[:, h * LANES:(h + 1) * LANES], gn_ref[...])
        pair = rot[:, (h // 2) * LANES:(h // 2 + 1) * LANES]
        keep = (lane < MLA_QK_ROPE) if h % 2 == 0 else (lane >= MLA_QK_ROPE)
        q_ref[:, h * 2 * LANES:h * 2 * LANES + LANES] = qn.astype(BF16)
        q_ref[:, h * 2 * LANES + LANES:(h + 1) * 2 * LANES] = jnp.where(keep, pair, zero).astype(BF16)


def _b_proj(h, g, wdq, cqg, wuq, gn, gp, gps, cos_t, sin_t, bd):
    t, d = h.shape
    tm = PROJ_TM
    qw = MLA_HEADS * 2 * LANES
    return pl.pallas_call(
        _b_proj_kernel,
        grid=(t // tm,),
        in_specs=[
            pl.BlockSpec((tm, d), lambda i: (i, 0)),
            _resident(g.shape), _resident(wdq.shape), _resident(cqg.shape), _resident(wuq.shape),
            _resident(gn.shape), _resident(gp.shape), _resident(gps.shape),
            pl.BlockSpec((tm, LANES), lambda i: (i, 0)),
            pl.BlockSpec((tm, LANES), lambda i: (i, 0)),
            _resident(bd.shape),
        ],
        out_specs=pl.BlockSpec((tm, qw), lambda i: (i, 0)),
        out_shape=jax.ShapeDtypeStruct((t, qw), BF16),
        compiler_params=_cparams(("parallel",)),
        name="b_proj",
    )(h, g, wdq, cqg, wuq, gn, gp, gps, cos_t, sin_t, bd)


def _b_attn_kernel(q_ref, k_ref, vt_ref, mask_ref, o_ref, m_sc, l_sc, acc_sc, s_sc, *, t, g):
    qi = pl.program_id(2)
    qw = 2 * LANES
    qs = [q_ref[:, gi * qw:(gi + 1) * qw] for gi in range(g)]
    _flash_init(m_sc, l_sc, acc_sc)

    def qk(slot, kj):
        for gi in range(g):
            k = k_ref[0, gi, pl.ds(pl.multiple_of(kj * t, t), t), :]
            _scores(k, qs[gi], s_sc.at[slot, gi])

    def sm(slot, kj, kind):
        for gi in range(g):
            extra = None if kind is None else mask_ref[...]
            _softmax_pv(s_sc.at[slot, gi], extra, vt_ref[0, gi, kj],
                        m_sc.at[gi], l_sc.at[gi], acc_sc.at[gi])

    odd = qi % 2 == 1
    _causal_pipeline(qi, qk, sm, [
        (jnp.logical_not(odd), [(qi, 0)]),
        (odd, [(qi - 1, None), (qi, 0)]),
    ])
    for gi in range(g):
        o_t = acc_sc[gi] * (1.0 / l_sc[gi])
        o_ref[:, gi * MLA_V_DIM:(gi + 1) * MLA_V_DIM] = o_t.T.astype(BF16)


def _b_attn(q, k, vt, mask, *, batch, seq):
    nh = MLA_HEADS
    t = B_T
    g = B_G
    nq = seq // t
    return pl.pallas_call(
        functools.partial(_b_attn_kernel, t=t, g=g),
        grid=(batch, nh // g, nq),
        in_specs=[
            pl.BlockSpec((t, g * 2 * LANES), lambda b, h, i: (b * nq + i, h)),
            pl.BlockSpec((1, g, seq, 2 * LANES), lambda b, h, i: (b, h, 0, 0)),
            pl.BlockSpec((1, g, nq, MLA_V_DIM, t), lambda b, h, i: (b, h, 0, 0, 0)),
            pl.BlockSpec(mask.shape, lambda b, h, i: (0, 0)),
        ],
        out_specs=pl.BlockSpec((t, g * MLA_V_DIM), lambda b, h, i: (b * nq + i, h)),
        out_shape=jax.ShapeDtypeStruct((batch * seq, nh * MLA_V_DIM), BF16),
        scratch_shapes=[
            pltpu.VMEM((g, 1, t), F32), pltpu.VMEM((g, 1, t), F32), pltpu.VMEM((g, MLA_V_DIM, t), F32),
            pltpu.VMEM((2, g, t, t), F32),
        ],
        compiler_params=_cparams(("parallel", "parallel", "arbitrary")),
        name="b_attn",
    )(q, k, vt, mask)


def _post_kernel(h_ref, o_ref, wo_ref, fg_ref, win_ref, cw_ref, cb_ref, wout_ref, pg_ref, wg_ref,
                 p_ref, wp_ref, out_ref, carry_sc, *, tiles_per_batch, d_ff):
    tm = h_ref.shape[0]
    h1 = h_ref[...] + jnp.dot(o_ref[...], wo_ref[...], preferred_element_type=F32)
    hn = _rms_rows(h1, fg_ref[...]).astype(BF16)

    @pl.when((pl.program_id(0) % tiles_per_batch) == 0)
    def _():
        carry_sc[...] = jnp.zeros(carry_sc.shape, F32)

    def conv(lo, width):
        u = jnp.dot(hn, win_ref[:, lo:lo + width], preferred_element_type=F32)
        prev = carry_sc[:, lo:lo + width]
        carry_sc[:, lo:lo + width] = u[tm - 8:, :]
        ext = jnp.concatenate([prev, u], axis=0)
        cw = cw_ref[:, lo:lo + width]
        return (cb_ref[:, lo:lo + width] + ext[6:6 + tm] * cw[0:1] + ext[7:7 + tm] * cw[1:2]
                + u * cw[2:3])

    acc = jnp.zeros((tm, out_ref.shape[1]), F32)
    lo = 0
    while lo < d_ff:
        width = min(FF_CHUNK, d_ff - lo)
        a = conv(lo, width)
        gt = conv(d_ff + lo, width)
        act = (gt * jax.nn.sigmoid(gt) * a).astype(BF16)
        acc = acc + jnp.dot(act, wout_ref[lo:lo + width, :], preferred_element_type=F32)
        lo += width
    h2 = h1 + acc
    gate = jax.nn.sigmoid(jnp.dot(_rms_rows(h2, pg_ref[...]).astype(BF16), wg_ref[...],
                                  preferred_element_type=F32))
    pe = jnp.dot(p_ref[...].astype(BF16), wp_ref[...], preferred_element_type=F32)
    out_ref[...] = h2 + pe * gate


def _post(h, o, wo, fg, win, cw, cb, wout, pg, wg, p_i, wp, *, seq):
    t, d = h.shape
    tm = POST_TM
    d_ff = wout.shape[0]
    return pl.pallas_call(
        functools.partial(_post_kernel, tiles_per_batch=seq // tm, d_ff=d_ff),
        grid=(t // tm,),
        in_specs=[
            pl.BlockSpec((tm, d), lambda i: (i, 0)),
            pl.BlockSpec((tm, o.shape[1]), lambda i: (i, 0)),
            _resident(wo.shape), _resident(fg.shape), _resident(win.shape), _resident(cw.shape),
            _resident(cb.shape), _resident(wout.shape), _resident(pg.shape), _resident(wg.shape),
            pl.BlockSpec((tm, p_i.shape[1]), lambda i: (i, 0)),
            _resident(wp.shape),
        ],
        out_specs=pl.BlockSpec((tm, d), lambda i: (i, 0)),
        out_shape=jax.ShapeDtypeStruct((t, d), F32),
        scratch_shapes=[pltpu.VMEM((8, 2 * d_ff), F32)],
        compiler_params=_cparams(("arbitrary",)),
        name="post",
    )(h, o, wo, fg, win, cw, cb, wout, pg, wg, p_i, wp)


def _t5_bucket_table():
    n = np.arange(T5_MAX_DISTANCE + 1)
    max_exact = T5_BUCKETS // 2
    log_ratio = np.log(np.maximum(n, 1) / max_exact) / math.log(T5_MAX_DISTANCE / max_exact)
    large = np.minimum(max_exact + (log_ratio * (T5_BUCKETS - max_exact)).astype(np.int32),
                       T5_BUCKETS - 1)
    return np.where(n < max_exact, n, large).astype(np.int32)


def _bias_tiles(rel_table, t):
    rel = rel_table[_t5_bucket_table()]
    rel = rel - rel[T5_MAX_DISTANCE]
    kk = np.arange(t)[:, None]
    qq = np.arange(t)[None, :]
    d0 = qq - kk
    d1 = t + qq - kk
    b0 = jnp.where((d0 >= 0)[..., None], rel[np.clip(d0, 0, T5_MAX_DISTANCE)], NEG_INF)
    b1 = rel[np.clip(d1, 0, T5_MAX_DISTANCE)]
    tiles = jnp.stack([b0, b1], axis=0).transpose(3, 0, 1, 2)
    return jnp.concatenate([tiles, tiles], axis=-1).astype(F32)


def _causal_tile(t):
    kk = np.arange(t)[:, None]
    qq = np.arange(t)[None, :]
    return jnp.asarray(np.where(qq >= kk, 0.0, NEG_INF), F32)


def _block_diag_ones(group):
    idx = np.arange(MXU_DIM) // group
    return jnp.asarray(idx[:, None] == idx[None, :], BF16)


def _swap_halves(v):
    half = v.shape[-1] // 2
    return jnp.concatenate([v[..., half:], v[..., :half]], axis=-1)


def kernel(x, p, positions, rel_bias_table, attn_norm, a_w_qkv, a_q_norm, a_k_norm, a_lam_q1, a_lam_k1, a_lam_q2, a_lam_k2, a_sub_norm, a_w_o, kv_norm, w_dkv, ckv_norm, w_ukv, k_nope_norm, k_pe_norm, b_w_dq, b_cq_norm, b_w_uq, b_q_nope_norm, b_q_pe_norm, b_w_o, ffn_norm, ffn_w_in, ffn_conv_w, ffn_conv_b, ffn_w_out, ple_norm, ple_w_proj, ple_w_gate):
    batch, seq, d = x.shape
    depth = p.shape[0]
    n_a = a_w_qkv.shape[0]
    t = batch * seq
    row = lambda v: v.reshape(1, -1).astype(F32)

    h = x.reshape(t, d)
    bd64 = _block_diag_ones(DIFF_QK_DIM)
    bias = _bias_tiles(rel_bias_table, A_T) * LOG2E
    cos_t, sin_t = _rope_tables(positions.reshape(t, 1))
    shared = None

    for i in range(depth):
        if i < n_a:
            lam_init = 0.8 - 0.6 * math.exp(-0.3 * i)
            q_scale = DIFF_QK_DIM ** -0.5 * LOG2E
            gqk = jnp.concatenate([jnp.tile(a_q_norm[i], 2 * DIFF_HEADS) * q_scale,
                                   jnp.tile(a_k_norm[i], 2 * DIFF_HEADS)]).reshape(1, -1)
            q, k, vt = _a_proj(h, row(attn_norm[i]), a_w_qkv[i].astype(BF16), gqk, bd64,
                               batch=batch, seq=seq, tk=A_T)
            lam_params = jnp.stack([a_lam_q1[i], a_lam_k1[i], a_lam_q2[i], a_lam_k2[i]])
            o = _a_attn(q, k, vt, bias, lam_params, row(a_sub_norm[i]), lam_init=lam_init)
            o = o.reshape(t, -1)
            w_o = a_w_o[i]
        else:
            j = i - n_a
            if shared is None:
                lora = ckv_norm.shape[0]
                w_pe = w_dkv[:, lora:]
                w_pe_sw = _swap_halves(w_pe)
                wd = jnp.concatenate([w_dkv[:, :lora], w_pe, w_pe, w_pe_sw, w_pe_sw], axis=1)
                wu = w_ukv.reshape(lora, MLA_HEADS, MLA_QK_NOPE + MLA_V_DIM)
                wu = jnp.concatenate([wu[:, :, :MLA_QK_NOPE].reshape(lora, -1),
                                      wu[:, :, MLA_QK_NOPE:].reshape(lora, -1)], axis=1)
                shared = _kv_proj(h, row(kv_norm), wd.astype(BF16), row(ckv_norm),
                                  row(jnp.tile(k_pe_norm, 2)), row(jnp.tile(_swap_halves(k_pe_norm), 2)),
                                  cos_t, sin_t, wu.astype(BF16), row(k_nope_norm),
                                  batch=batch, seq=seq, tk=B_T)
                mask = _causal_tile(B_T)
            q_lora = b_w_uq.shape[1]
            wq = b_w_uq[j].reshape(q_lora, MLA_HEADS, MLA_QK_NOPE + MLA_QK_ROPE)
            wq_pe = wq[:, :, MLA_QK_NOPE:]
            wuq = jnp.concatenate([wq[:, :, :MLA_QK_NOPE].reshape(q_lora, -1),
                                   wq_pe.reshape(q_lora, -1),
                                   _swap_halves(wq_pe).reshape(q_lora, -1)], axis=1)
            scale = (MLA_QK_NOPE + MLA_QK_ROPE) ** -0.5 * LOG2E
            gp = b_q_pe_norm[j] * scale
            qf = _b_proj(h, row(attn_norm[i]), b_w_dq[j].astype(BF16), row(b_cq_norm[j]),
                         wuq.astype(BF16), row(b_q_nope_norm[j] * scale),
                         row(jnp.tile(gp, MLA_HEADS)), row(jnp.tile(_swap_halves(gp), MLA_HEADS)),
                         cos_t, sin_t, bd64)
            o = _b_attn(qf, shared[0], shared[1], mask, batch=batch, seq=seq)
            w_o = b_w_o[j]
        h = _post(h, o, w_o.astype(BF16), row(ffn_norm[i]), ffn_w_in[i].astype(BF16),
                  ffn_conv_w[i], row(ffn_conv_b[i]), ffn_w_out[i].astype(BF16), row(ple_norm[i]),
                  ple_w_gate[i].astype(BF16), p[i].reshape(t, -1), ple_w_proj[i].astype(BF16),
                  seq=seq)
    return h.reshape(batch, seq, d)
```

```python
import functools
import math

import numpy as np
import jax
import jax.numpy as jnp
from jax import lax
from jax.experimental import pallas as pl
from jax.experimental.pallas import tpu as pltpu

F32 = jnp.float32
BF16 = jnp.bfloat16

DIFF_HEADS = 8
DIFF_QK_DIM = 64
DIFF_V_DIM = 128
T5_BUCKETS = 32
T5_MAX_DISTANCE = 128
MLA_HEADS = 16
MLA_QK_NOPE = 128
MLA_QK_ROPE = 64
MLA_V_DIM = 128
ROPE_BASE = 10000.0
RMS_EPS = 1e-6
NEG_INF = -1e30
LOG2E = math.log2(math.e)

LANES = 128
MXU_DIM = 256
BF16_SUBLANES = 16
VT_ROWS = DIFF_V_DIM + BF16_SUBLANES
VMEM_LIMIT_BYTES = 56 * 1024 * 1024

PROJ_TM = 512
POST_TM = 256
A_T = 256
A_G = 4
B_T = 512
B_G = 4
FF_CHUNK = 768


def _cparams(sem):
    return pltpu.CompilerParams(dimension_semantics=sem, vmem_limit_bytes=VMEM_LIMIT_BYTES)


def _resident(shape):
    nd = len(shape)
    return pl.BlockSpec(shape, lambda *_: (0,) * nd, pipeline_mode=pl.Buffered(1))


def _rms_rows(x, g):
    ms = jnp.mean(x * x, axis=-1, keepdims=True)
    return x * lax.rsqrt(ms + RMS_EPS) * g


def _store_vt(vt_ref, v):
    vdim = v.shape[1]
    vt_ref[:vdim, :] = v.T.astype(BF16)
    vt_ref[vdim:, :] = jnp.ones((VT_ROWS - vdim, v.shape[0]), BF16)


def _group_mean_sq(x, bd, group):
    y = (x * x).astype(BF16)
    parts = [jnp.dot(y[:, c:c + MXU_DIM], bd, preferred_element_type=F32)
             for c in range(0, x.shape[1], MXU_DIM)]
    return jnp.concatenate(parts, axis=1) * (1.0 / group)


def _rope_kernel(pos_ref, cos_ref, sin_ref):
    pos = pos_ref[...].astype(F32)
    lane = lax.broadcasted_iota(jnp.int32, (1, LANES), 1)
    half = MLA_QK_ROPE // 2
    r = (lane % half).astype(F32)
    inv_freq = jnp.exp(-math.log(ROPE_BASE) * r * (2.0 / MLA_QK_ROPE))
    ang = pos * inv_freq
    cos_ref[...] = jnp.cos(ang)
    sin_ref[...] = jnp.where((lane % MLA_QK_ROPE) < half, -jnp.sin(ang), jnp.sin(ang))


def _rope_tables(pos_col):
    t = pos_col.shape[0]
    tm = 2048
    return pl.pallas_call(
        _rope_kernel,
        grid=(t // tm,),
        in_specs=[pl.BlockSpec((tm, 1), lambda i: (i, 0))],
        out_specs=[pl.BlockSpec((tm, LANES), lambda i: (i, 0))] * 2,
        out_shape=[jax.ShapeDtypeStruct((t, LANES), F32)] * 2,
        compiler_params=_cparams(("parallel",)),
        name="rope_tables",
    )(pos_col)


def _a_proj_kernel(h_ref, g_ref, w_ref, gqk_ref, bd_ref, q_ref, k_ref, vt_ref, *, tk):
    tm = h_ref.shape[0]
    qk_w = DIFF_HEADS * 2 * DIFF_QK_DIM
    hn = _rms_rows(h_ref[...], g_ref[...]).astype(BF16)
    qkv = jnp.dot(hn, w_ref[...], preferred_element_type=F32)
    qk = qkv[:, :2 * qk_w]
    ms = _group_mean_sq(qk, bd_ref[...], DIFF_QK_DIM)
    qkn = qk * lax.rsqrt(ms + RMS_EPS) * gqk_ref[...]
    for h in range(DIFF_HEADS):
        q_ref[0, h] = qkn[:, h * LANES:(h + 1) * LANES].astype(BF16)
        k_ref[0, h] = qkn[:, qk_w + h * LANES:qk_w + (h + 1) * LANES].astype(BF16)
        v_h = qkv[:, 2 * qk_w + h * LANES:2 * qk_w + (h + 1) * LANES]
        for c in range(tm // tk):
            _store_vt(vt_ref.at[0, h, c], v_h[c * tk:(c + 1) * tk])


def _a_proj(h, g, w_qkv, gqk, bd, *, batch, seq, tk):
    tm = PROJ_TM
    nt = seq // tm
    nh = DIFF_HEADS
    d = h.shape[1]
    return pl.pallas_call(
        functools.partial(_a_proj_kernel, tk=tk),
        grid=(batch * nt,),
        in_specs=[
            pl.BlockSpec((tm, d), lambda i: (i, 0)),
            _resident(g.shape), _resident(w_qkv.shape), _resident(gqk.shape), _resident(bd.shape),
        ],
        out_specs=[
            pl.BlockSpec((1, nh, tm, LANES), lambda i: (i // nt, 0, i % nt, 0)),
            pl.BlockSpec((1, nh, tm, LANES), lambda i: (i // nt, 0, i % nt, 0)),
            pl.BlockSpec((1, nh, tm // tk, VT_ROWS, tk), lambda i: (i // nt, 0, i % nt, 0, 0)),
        ],
        out_shape=[
            jax.ShapeDtypeStruct((batch, nh, seq, LANES), BF16),
            jax.ShapeDtypeStruct((batch, nh, seq, LANES), BF16),
            jax.ShapeDtypeStruct((batch, nh, seq // tk, VT_ROWS, tk), BF16),
        ],
        compiler_params=_cparams(("parallel",)),
        name="a_proj",
    )(h, g, w_qkv, gqk, bd)


def _scores(k, qz, s_ref):
    s_ref[...] = lax.dot_general(k, qz, (((1,), (1,)), ((), ())), preferred_element_type=F32)


def _softmax_pv(s_ref, extra, vt, m_sc, acc_sc):
    s = s_ref[...]
    if extra is not None:
        s = s + extra
    m_old = m_sc[...]
    m_new = jnp.maximum(m_old, jnp.max(s, axis=0, keepdims=True))
    alpha = jnp.exp2(m_old - m_new)
    p = jnp.exp2(s - m_new)
    acc_sc[...] = alpha * acc_sc[...] + jnp.dot(vt, p.astype(BF16), preferred_element_type=F32)
    m_sc[...] = m_new


def _flash_init(m_sc, acc_sc):
    m_sc[...] = jnp.full(m_sc.shape, NEG_INF, F32)
    acc_sc[...] = jnp.zeros(acc_sc.shape, F32)


def _causal_pipeline(n_far, qk, sm, tails):
    qk(0, 0)

    def body(jj, carry):
        kj = 2 * jj
        qk(1, kj + 1)
        sm(0, kj, None)
        qk(0, kj + 2)
        sm(1, kj + 1, None)
        return carry

    lax.fori_loop(0, n_far // 2, body, 0)

    for cond, blocks in tails:
        @pl.when(cond)
        def _(blocks=blocks):
            for n, (kj, kind) in enumerate(blocks):
                if n + 1 < len(blocks):
                    qk(1 - n % 2, blocks[n + 1][0])
                sm(n % 2, kj, kind)


def _a_attn_kernel(q_ref, k_ref, vt_ref, bias_ref, lam_ref, sg_ref, o_ref, m_sc, acc_sc, s_sc,
                   *, lam_init, t, g):
    qi = pl.program_id(2)
    qz = []
    for gi in range(g):
        q = q_ref[0, gi]
        lane = lax.broadcasted_iota(jnp.int32, q.shape, 1)
        zero = jnp.zeros_like(q)
        qz.append(jnp.concatenate([jnp.where(lane < DIFF_QK_DIM, q, zero),
                                   jnp.where(lane >= DIFF_QK_DIM, q, zero)], axis=0))
    _flash_init(m_sc, acc_sc)

    def qk(slot, kj):
        for gi in range(g):
            k = k_ref[0, gi, pl.ds(pl.multiple_of(kj * t, t), t), :]
            _scores(k, qz[gi], s_sc.at[slot, gi])

    def sm(slot, kj, kind):
        for gi in range(g):
            extra = None if kind is None else bias_ref[gi, kind]
            _softmax_pv(s_sc.at[slot, gi], extra, vt_ref[0, gi, kj], m_sc.at[gi], acc_sc.at[gi])

    n_far = jnp.maximum(qi - 1, 0)
    odd = n_far % 2 == 1
    _causal_pipeline(n_far, qk, sm, [
        (qi == 0, [(qi, 0)]),
        ((qi >= 1) & jnp.logical_not(odd), [(qi - 1, 1), (qi, 0)]),
        (odd, [(qi - 2, None), (qi - 1, 1), (qi, 0)]),
    ])

    lp = lam_ref[...]
    lam = (jnp.exp(jnp.sum(lp[0:1] * lp[1:2], keepdims=True))
           - jnp.exp(jnp.sum(lp[2:3] * lp[3:4], keepdims=True)) + lam_init)
    for gi in range(g):
        acc = acc_sc[gi, :DIFF_V_DIM]
        inv = 1.0 / acc_sc[gi, DIFF_V_DIM:DIFF_V_DIM + 1]
        o_t = acc[:, :t] * inv[:, :t] - lam * (acc[:, t:] * inv[:, t:])
        o = _rms_rows(o_t.T, sg_ref[...]) * (1.0 - lam_init)
        o_ref[0, :, gi * DIFF_V_DIM:(gi + 1) * DIFF_V_DIM] = o.astype(BF16)


def _a_attn(q, k, vt, bias, lam_params, sub_gain, *, lam_init):
    batch, nh, seq, _ = q.shape
    t = A_T
    g = A_G
    nq = seq // t
    return pl.pallas_call(
        functools.partial(_a_attn_kernel, lam_init=lam_init, t=t, g=g),
        grid=(batch, nh // g, nq),
        in_specs=[
            pl.BlockSpec((1, g, t, LANES), lambda b, h, i: (b, h, i, 0)),
            pl.BlockSpec((1, g, seq, LANES), lambda b, h, i: (b, h, 0, 0)),
            pl.BlockSpec((1, g, nq, VT_ROWS, t), lambda b, h, i: (b, h, 0, 0, 0)),
            pl.BlockSpec((g, 2, t, 2 * t), lambda b, h, i: (h, 0, 0, 0)),
            pl.BlockSpec(lam_params.shape, lambda b, h, i: (0, 0)),
            pl.BlockSpec(sub_gain.shape, lambda b, h, i: (0, 0)),
        ],
        out_specs=pl.BlockSpec((1, t, g * DIFF_V_DIM), lambda b, h, i: (b, i, h)),
        out_shape=jax.ShapeDtypeStruct((batch, seq, nh * DIFF_V_DIM), BF16),
        scratch_shapes=[
            pltpu.VMEM((g, 1, 2 * t), F32), pltpu.VMEM((g, VT_ROWS, 2 * t), F32),
            pltpu.VMEM((2, g, t, 2 * t), F32),
        ],
        compiler_params=_cparams(("parallel", "parallel", "arbitrary")),
        name="a_attn",
    )(q, k, vt, bias, lam_params, sub_gain)


def _kv_proj_kernel(h_ref, g_ref, wd_ref, cg_ref, pg_ref, pgs_ref, cos_ref, sin_ref, wu_ref, kg_ref,
                    k_ref, vt_ref, *, tk):
    tm = h_ref.shape[0]
    lora = cg_ref.shape[1]
    hn = _rms_rows(h_ref[...], g_ref[...]).astype(BF16)
    ckv = jnp.dot(hn, wd_ref[...], preferred_element_type=F32)
    c_kv = _rms_rows(ckv[:, :lora], cg_ref[...]).astype(BF16)
    pe = ckv[:, lora:lora + LANES]
    pe_sw = ckv[:, lora + LANES:]
    r = lax.rsqrt(jnp.mean(pe * pe, axis=-1, keepdims=True) + RMS_EPS)
    k_pe = ((pe * r * pg_ref[...]) * cos_ref[...] + (pe_sw * r * pgs_ref[...]) * sin_ref[...])
    lane = lax.broadcasted_iota(jnp.int32, k_pe.shape, 1)
    zero = jnp.zeros_like(k_pe)
    pe_even = jnp.where(lane < MLA_QK_ROPE, k_pe, zero).astype(BF16)
    pe_odd = jnp.where(lane >= MLA_QK_ROPE, k_pe, zero).astype(BF16)
    kv = jnp.dot(c_kv, wu_ref[...], preferred_element_type=F32)
    nope_w = MLA_HEADS * MLA_QK_NOPE
    for h in range(MLA_HEADS):
        kn = _rms_rows(kv[:, h * LANES:(h + 1) * LANES], kg_ref[...])
        k_ref[0, h, :, :LANES] = kn.astype(BF16)
        k_ref[0, h, :, LANES:] = pe_even if h % 2 == 0 else pe_odd
        v_h = kv[:, nope_w + h * LANES:nope_w + (h + 1) * LANES]
        for c in range(tm // tk):
            _store_vt(vt_ref.at[0, h, c], v_h[c * tk:(c + 1) * tk])


def _kv_proj(h, g, wd, cg, pg, pgs, cos_t, sin_t, wu, kg, *, batch, seq, tk):
    tm = PROJ_TM
    nt = seq // tm
    nh = MLA_HEADS
    d = h.shape[1]
    return pl.pallas_call(
        functools.partial(_kv_proj_kernel, tk=tk),
        grid=(batch * nt,),
        in_specs=[
            pl.BlockSpec((tm, d), lambda i: (i, 0)),
            _resident(g.shape), _resident(wd.shape), _resident(cg.shape),
            _resident(pg.shape), _resident(pgs.shape),
            pl.BlockSpec((tm, LANES), lambda i: (i, 0)),
            pl.BlockSpec((tm, LANES), lambda i: (i, 0)),
            _resident(wu.shape), _resident(kg.shape),
        ],
        out_specs=[
            pl.BlockSpec((1, nh, tm, 2 * LANES), lambda i: (i // nt, 0, i % nt, 0)),
            pl.BlockSpec((1, nh, tm // tk, VT_ROWS, tk), lambda i: (i // nt, 0, i % nt, 0, 0)),
        ],
        out_shape=[
            jax.ShapeDtypeStruct((batch, nh, seq, 2 * LANES), BF16),
            jax.ShapeDtypeStruct((batch, nh, seq // tk, VT_ROWS, tk), BF16),
        ],
        compiler_params=_cparams(("parallel",)),
        name="kv_proj",
    )(h, g, wd, cg, pg, pgs, cos_t, sin_t, wu, kg)


def _b_proj_kernel(h_ref, g_ref, wdq_ref, cqg_ref, wuq_ref, gn_ref, gp_ref, gps_ref, cos_ref, sin_ref,
                   bd_ref, q_ref):
    nope_w = MLA_HEADS * MLA_QK_NOPE
    pe_w = MLA_HEADS * MLA_QK_ROPE
    hn = _rms_rows(h_ref[...], g_ref[...]).astype(BF16)
    cq = jnp.dot(hn, wdq_ref[...], preferred_element_type=F32)
    cqn = _rms_rows(cq, cqg_ref[...]).astype(BF16)
    q = jnp.dot(cqn, wuq_ref[...], preferred_element_type=F32)
    qp = q[:, nope_w:nope_w + pe_w]
    qps = q[:, nope_w + pe_w:]
    r = lax.rsqrt(_group_mean_sq(qp, bd_ref[...], MLA_QK_ROPE) + RMS_EPS)
    reps = pe_w // LANES
    cos_t = jnp.concatenate([cos_ref[...]] * reps, axis=1)
    sin_t = jnp.concatenate([sin_ref[...]] * reps, axis=1)
    rot = (qp * r * gp_ref[...]) * cos_t + (qps * r * gps_ref[...]) * sin_t
    lane = lax.broadcasted_iota(jnp.int32, (q.shape[0], LANES), 1)
    zero = jnp.zeros((q.shape[0], LANES), F32)
    for h in range(MLA_HEADS):
        qn = _rms_rows(q[:, h * LANES:(h + 1) * LANES], gn_ref[...])
        pair = rot[:, (h // 2) * LANES:(h // 2 + 1) * LANES]
        keep = (lane < MLA_QK_ROPE) if h % 2 == 0 else (lane >= MLA_QK_ROPE)
        q_ref[:, h * 2 * LANES:h * 2 * LANES + LANES] = qn.astype(BF16)
        q_ref[:, h * 2 * LANES + LANES:(h + 1) * 2 * LANES] = jnp.where(keep, pair, zero).astype(BF16)


def _b_proj(h, g, wdq, cqg, wuq, gn, gp, gps, cos_t, sin_t, bd):
    t, d = h.shape
    tm = PROJ_TM
    qw = MLA_HEADS * 2 * LANES
    return pl.pallas_call(
        _b_proj_kernel,
        grid=(t // tm,),
        in_specs=[
            pl.BlockSpec((tm, d), lambda i: (i, 0)),
            _resident(g.shape), _resident(wdq.shape), _resident(cqg.shape), _resident(wuq.shape),
            _resident(gn.shape), _resident(gp.shape), _resident(gps.shape),
            pl.BlockSpec((tm, LANES), lambda i: (i, 0)),
            pl.BlockSpec((tm, LANES), lambda i: (i, 0)),
            _resident(bd.shape),
        ],
        out_specs=pl.BlockSpec((tm, qw), lambda i: (i, 0)),
        out_shape=jax.ShapeDtypeStruct((t, qw), BF16),
        compiler_params=_cparams(("parallel",)),
        name="b_proj",
    )(h, g, wdq, cqg, wuq, gn, gp, gps, cos_t, sin_t, bd)


def _b_attn_kernel(q_ref, k_ref, vt_ref, mask_ref, o_ref, m_sc, acc_sc, s_sc, *, t, g):
    qi = pl.program_id(2)
    qw = 2 * LANES
    qs = [q_ref[:, gi * qw:(gi + 1) * qw] for gi in range(g)]
    _flash_init(m_sc, acc_sc)

    def qk(slot, kj):
        for gi in range(g):
            k = k_ref[0, gi, pl.ds(pl.multiple_of(kj * t, t), t), :]
            _scores(k, qs[gi], s_sc.at[slot, gi])

    def sm(slot, kj, kind):
        for gi in range(g):
            extra = None if kind is None else mask_ref[...]
            _softmax_pv(s_sc.at[slot, gi], extra, vt_ref[0, gi, kj], m_sc.at[gi], acc_sc.at[gi])

    odd = qi % 2 == 1
    _causal_pipeline(qi, qk, sm, [
        (jnp.logical_not(odd), [(qi, 0)]),
        (odd, [(qi - 1, None), (qi, 0)]),
    ])
    for gi in range(g):
        o_t = acc_sc[gi, :MLA_V_DIM] * (1.0 / acc_sc[gi, MLA_V_DIM:MLA_V_DIM + 1])
        o_ref[:, gi * MLA_V_DIM:(gi + 1) * MLA_V_DIM] = o_t.T.astype(BF16)


def _b_attn(q, k, vt, mask, *, batch, seq):
    nh = MLA_HEADS
    t = B_T
    g = B_G
    nq = seq // t
    return pl.pallas_call(
        functools.partial(_b_attn_kernel, t=t, g=g),
        grid=(batch, nh // g, nq),
        in_specs=[
            pl.BlockSpec((t, g * 2 * LANES), lambda b, h, i: (b * nq + i, h)),
            pl.BlockSpec((1, g, seq, 2 * LANES), lambda b, h, i: (b, h, 0, 0)),
            pl.BlockSpec((1, g, nq, VT_ROWS, t), lambda b, h, i: (b, h, 0, 0, 0)),
            pl.BlockSpec(mask.shape, lambda b, h, i: (0, 0)),
        ],
        out_specs=pl.BlockSpec((t, g * MLA_V_DIM), lambda b, h, i: (b * nq + i, h)),
        out_shape=jax.ShapeDtypeStruct((batch * seq, nh * MLA_V_DIM), BF16),
        scratch_shapes=[
            pltpu.VMEM((g, 1, t), F32), pltpu.VMEM((g, VT_ROWS, t), F32),
            pltpu.VMEM((2, g, t, t), F32),
        ],
        compiler_params=_cparams(("parallel", "parallel", "arbitrary")),
        name="b_attn",
    )(q, k, vt, mask)


def _post_kernel(h_ref, o_ref, wo_ref, fg_ref, win_ref, cw_ref, cb_ref, wout_ref, pg_ref, wg_ref,
                 p_ref, wp_ref, out_ref, carry_sc, *, tiles_per_batch, d_ff):
    tm = h_ref.shape[0]
    h1 = h_ref[...] + jnp.dot(o_ref[...], wo_ref[...], preferred_element_type=F32)
    hn = _rms_rows(h1, fg_ref[...]).astype(BF16)

    @pl.when((pl.program_id(0) % tiles_per_batch) == 0)
    def _():
        carry_sc[...] = jnp.zeros(carry_sc.shape, F32)

    def conv(lo, width):
        u = jnp.dot(hn, win_ref[:, lo:lo + width], preferred_element_type=F32)
        prev = carry_sc[:, lo:lo + width]
        carry_sc[:, lo:lo + width] = u[tm - 8:, :]
        ext = jnp.concatenate([prev, u], axis=0)
        cw = cw_ref[:, lo:lo + width]
        return (cb_ref[:, lo:lo + width] + ext[6:6 + tm] * cw[0:1] + ext[7:7 + tm] * cw[1:2]
                + u * cw[2:3])

    acc = jnp.zeros((tm, out_ref.shape[1]), F32)
    lo = 0
    while lo < d_ff:
        width = min(FF_CHUNK, d_ff - lo)
        a = conv(lo, width)
        gt = conv(d_ff + lo, width)
        act = (gt * jax.nn.sigmoid(gt) * a).astype(BF16)
        acc = acc + jnp.dot(act, wout_ref[lo:lo + width, :], preferred_element_type=F32)
        lo += width
    h2 = h1 + acc
    gate = jax.nn.sigmoid(jnp.dot(_rms_rows(h2, pg_ref[...]).astype(BF16), wg_ref[...],
                                  preferred_element_type=F32))
    pe = jnp.dot(p_ref[...].astype(BF16), wp_ref[...], preferred_element_type=F32)
    out_ref[...] = h2 + pe * gate


def _post(h, o, wo, fg, win, cw, cb, wout, pg, wg, p_i, wp, *, seq):
    t, d = h.shape
    tm = POST_TM
    d_ff = wout.shape[0]
    return pl.pallas_call(
        functools.partial(_post_kernel, tiles_per_batch=seq // tm, d_ff=d_ff),
        grid=(t // tm,),
        in_specs=[
            pl.BlockSpec((tm, d), lambda i: (i, 0)),
            pl.BlockSpec((tm, o.shape[1]), lambda i: (i, 0)),
            _resident(wo.shape), _resident(fg.shape), _resident(win.shape), _resident(cw.shape),
            _resident(cb.shape), _resident(wout.shape), _resident(pg.shape), _resident(wg.shape),
            pl.BlockSpec((tm, p_i.shape[1]), lambda i: (i, 0)),
            _resident(wp.shape),
        ],
        out_specs=pl.BlockSpec((tm, d), lambda i: (i, 0)),
        out_shape=jax.ShapeDtypeStruct((t, d), F32),
        scratch_shapes=[pltpu.VMEM((8, 2 * d_ff), F32)],
        compiler_params=_cparams(("arbitrary",)),
        name="post",
    )(h, o, wo, fg, win, cw, cb, wout, pg, wg, p_i, wp)


def _t5_bucket_table():
    n = np.arange(T5_MAX_DISTANCE + 1)
    max_exact = T5_BUCKETS // 2
    log_ratio = np.log(np.maximum(n, 1) / max_exact) / math.log(T5_MAX_DISTANCE / max_exact)
    large = np.minimum(max_exact + (log_ratio * (T5_BUCKETS - max_exact)).astype(np.int32),
                       T5_BUCKETS - 1)
    return np.where(n < max_exact, n, large).astype(np.int32)


def _bias_tiles(rel_table, t):
    assert t >= T5_MAX_DISTANCE
    nh = rel_table.shape[1]
    rel = rel_table[_t5_bucket_table()]
    rel = (rel - rel[T5_MAX_DISTANCE]).astype(F32)
    near = rel[:T5_MAX_DISTANCE]
    zeros = jnp.zeros((t - T5_MAX_DISTANCE, nh), F32)

    def toeplitz(v_pos, v_neg):
        v = jnp.concatenate([v_pos, jnp.zeros((1, nh), F32), v_neg], axis=0)
        u = jnp.tile(v, (t, 1))[:t * (2 * t - 1)]
        return u.reshape(t, 2 * t - 1, nh)[:, :t]

    b0 = toeplitz(jnp.concatenate([near, zeros], axis=0), jnp.full((t - 1, nh), NEG_INF, F32))
    b1 = toeplitz(jnp.zeros((t, nh), F32), jnp.concatenate([near[1:], zeros], axis=0))
    tiles = jnp.stack([b0, b1], axis=0).transpose(3, 0, 1, 2)
    return jnp.concatenate([tiles, tiles], axis=-1)


def _causal_tile(t):
    kk = np.arange(t)[:, None]
    qq = np.arange(t)[None, :]
    return jnp.asarray(np.where(qq >= kk, 0.0, NEG_INF), F32)


def _block_diag_ones(group):
    idx = np.arange(MXU_DIM) // group
    return jnp.asarray(idx[:, None] == idx[None, :], BF16)


def _swap_halves(v):
    half = v.shape[-1] // 2
    return jnp.concatenate([v[..., half:], v[..., :half]], axis=-1)


def kernel(x, p, positions, rel_bias_table, attn_norm, a_w_qkv, a_q_norm, a_k_norm, a_lam_q1, a_lam_k1, a_lam_q2, a_lam_k2, a_sub_norm, a_w_o, kv_norm, w_dkv, ckv_norm, w_ukv, k_nope_norm, k_pe_norm, b_w_dq, b_cq_norm, b_w_uq, b_q_nope_norm, b_q_pe_norm, b_w_o, ffn_norm, ffn_w_in, ffn_conv_w, ffn_conv_b, ffn_w_out, ple_norm, ple_w_proj, ple_w_gate):
    batch, seq, d = x.shape
    depth = p.shape[0]
    n_a = a_w_qkv.shape[0]
    t = batch * seq
    row = lambda v: v.reshape(1, -1).astype(F32)

    h = x.reshape(t, d)
    bd64 = _block_diag_ones(DIFF_QK_DIM)
    bias = _bias_tiles(rel_bias_table, A_T) * LOG2E
    cos_t, sin_t = _rope_tables(positions.reshape(t, 1))
    shared = None

    for i in range(depth):
        if i < n_a:
            lam_init = 0.8 - 0.6 * math.exp(-0.3 * i)
            q_scale = DIFF_QK_DIM ** -0.5 * LOG2E
            gqk = jnp.concatenate([jnp.tile(a_q_norm[i], 2 * DIFF_HEADS) * q_scale,
                                   jnp.tile(a_k_norm[i], 2 * DIFF_HEADS)]).reshape(1, -1)
            q, k, vt = _a_proj(h, row(attn_norm[i]), a_w_qkv[i].astype(BF16), gqk, bd64,
                               batch=batch, seq=seq, tk=A_T)
            lam_params = jnp.stack([a_lam_q1[i], a_lam_k1[i], a_lam_q2[i], a_lam_k2[i]])
            o = _a_attn(q, k, vt, bias, lam_params, row(a_sub_norm[i]), lam_init=lam_init)
            o = o.reshape(t, -1)
            w_o = a_w_o[i]
        else:
            j = i - n_a
            if shared is None:
                lora = ckv_norm.shape[0]
                w_pe = w_dkv[:, lora:]
                w_pe_sw = _swap_halves(w_pe)
                wd = jnp.concatenate([w_dkv[:, :lora], w_pe, w_pe, w_pe_sw, w_pe_sw], axis=1)
                wu = w_ukv.reshape(lora, MLA_HEADS, MLA_QK_NOPE + MLA_V_DIM)
                wu = jnp.concatenate([wu[:, :, :MLA_QK_NOPE].reshape(lora, -1),
                                      wu[:, :, MLA_QK_NOPE:].reshape(lora, -1)], axis=1)
                shared = _kv_proj(h, row(kv_norm), wd.astype(BF16), row(ckv_norm),
                                  row(jnp.tile(k_pe_norm, 2)), row(jnp.tile(_swap_halves(k_pe_norm), 2)),
                                  cos_t, sin_t, wu.astype(BF16), row(k_nope_norm),
                                  batch=batch, seq=seq, tk=B_T)
                mask = _causal_tile(B_T)
            q_lora = b_w_uq.shape[1]
            wq = b_w_uq[j].reshape(q_lora, MLA_HEADS, MLA_QK_NOPE + MLA_QK_ROPE)
            wq_pe = wq[:, :, MLA_QK_NOPE:]
            wuq = jnp.concatenate([wq[:, :, :MLA_QK_NOPE].reshape(q_lora, -1),
                                   wq_pe.reshape(q_lora, -1),
                                   _swap_halves(wq_pe).reshape(q_lora, -1)], axis=1)
            scale = (MLA_QK_NOPE + MLA_QK_ROPE) ** -0.5 * LOG2E
            gp = b_q_pe_norm[j] * scale
            qf = _b_proj(h, row(attn_norm[i]), b_w_dq[j].astype(BF16), row(b_cq_norm[j]),
                         wuq.astype(BF16), row(b_q_nope_norm[j] * scale),
                         row(jnp.tile(gp, MLA_HEADS)), row(jnp.tile(_swap_halves(gp), MLA_HEADS)),
                         cos_t, sin_t, bd64)
            o = _b_attn(qf, shared[0], shared[1], mask, batch=batch, seq=seq)
            w_o = b_w_o[j]
        h = _post(h, o, w_o.astype(BF16), row(ffn_norm[i]), ffn_w_in[i].astype(BF16),
                  ffn_conv_w[i], row(ffn_conv_b[i]), ffn_w_out[i].astype(BF16), row(ple_norm[i]),
                  ple_w_gate[i].astype(BF16), p[i].reshape(t, -1), ple_w_proj[i].astype(BF16),
                  seq=seq)
    return h.reshape(batch, seq, d)
```

```python
import functools
import math

import numpy as np
import jax
import jax.numpy as jnp
from jax import lax
from jax.experimental import pallas as pl
from jax.experimental.pallas import tpu as pltpu

F32 = jnp.float32
BF16 = jnp.bfloat16

DIFF_HEADS = 8
DIFF_QK_DIM = 64
DIFF_V_DIM = 128
T5_BUCKETS = 32
T5_MAX_DISTANCE = 128
MLA_HEADS = 16
MLA_QK_NOPE = 128
MLA_QK_ROPE = 64
MLA_V_DIM = 128
ROPE_BASE = 10000.0
RMS_EPS = 1e-6
NEG_INF = -1e30
LOG2E = math.log2(math.e)

LANES = 128
MXU_DIM = 256
BF16_SUBLANES = 16
VT_ROWS = DIFF_V_DIM + BF16_SUBLANES
VMEM_LIMIT_BYTES = 56 * 1024 * 1024

PROJ_TM = 512
POST_TM = 512
A_T = 256
A_G = 4
B_T = 512
B_G = 4
FF_CHUNK = 768


def _cparams(sem):
    return pltpu.CompilerParams(dimension_semantics=sem, vmem_limit_bytes=VMEM_LIMIT_BYTES)


def _resident(shape):
    nd = len(shape)
    return pl.BlockSpec(shape, lambda *_: (0,) * nd, pipeline_mode=pl.Buffered(1))


def _rms_rows(x, g):
    ms = jnp.mean(x * x, axis=-1, keepdims=True)
    return x * lax.rsqrt(ms + RMS_EPS) * g


def _store_vt(vt_ref, v):
    vdim = v.shape[1]
    vt_ref[:vdim, :] = v.T.astype(BF16)
    vt_ref[vdim:, :] = jnp.ones((VT_ROWS - vdim, v.shape[0]), BF16)


def _group_mean_sq(x, bd, group):
    y = (x * x).astype(BF16)
    parts = [jnp.dot(y[:, c:c + MXU_DIM], bd, preferred_element_type=F32)
             for c in range(0, x.shape[1], MXU_DIM)]
    return jnp.concatenate(parts, axis=1) * (1.0 / group)


def _rope_kernel(pos_ref, cos_ref, sin_ref):
    pos = pos_ref[...].astype(F32)
    lane = lax.broadcasted_iota(jnp.int32, (1, LANES), 1)
    half = MLA_QK_ROPE // 2
    r = (lane % half).astype(F32)
    inv_freq = jnp.exp(-math.log(ROPE_BASE) * r * (2.0 / MLA_QK_ROPE))
    ang = pos * inv_freq
    cos_ref[...] = jnp.cos(ang)
    sin_ref[...] = jnp.where((lane % MLA_QK_ROPE) < half, -jnp.sin(ang), jnp.sin(ang))


def _rope_tables(pos_col):
    t = pos_col.shape[0]
    tm = 2048
    return pl.pallas_call(
        _rope_kernel,
        grid=(t // tm,),
        in_specs=[pl.BlockSpec((tm, 1), lambda i: (i, 0))],
        out_specs=[pl.BlockSpec((tm, LANES), lambda i: (i, 0))] * 2,
        out_shape=[jax.ShapeDtypeStruct((t, LANES), F32)] * 2,
        compiler_params=_cparams(("parallel",)),
        name="rope_tables",
    )(pos_col)


def _a_proj_kernel(h_ref, g_ref, w_ref, gqk_ref, bd_ref, q_ref, k_ref, vt_ref, *, tk):
    tm = h_ref.shape[0]
    qk_w = DIFF_HEADS * 2 * DIFF_QK_DIM
    hn = _rms_rows(h_ref[...], g_ref[...]).astype(BF16)
    qkv = jnp.dot(hn, w_ref[...], preferred_element_type=F32)
    qk = qkv[:, :2 * qk_w]
    ms = _group_mean_sq(qk, bd_ref[...], DIFF_QK_DIM)
    qkn = qk * lax.rsqrt(ms + RMS_EPS) * gqk_ref[...]
    for h in range(DIFF_HEADS):
        q_ref[0, h] = qkn[:, h * LANES:(h + 1) * LANES].astype(BF16)
        k_ref[0, h] = qkn[:, qk_w + h * LANES:qk_w + (h + 1) * LANES].astype(BF16)
        v_h = qkv[:, 2 * qk_w + h * LANES:2 * qk_w + (h + 1) * LANES]
        for c in range(tm // tk):
            _store_vt(vt_ref.at[0, h, c], v_h[c * tk:(c + 1) * tk])


def _a_proj(h, g, w_qkv, gqk, bd, *, batch, seq, tk):
    tm = PROJ_TM
    nt = seq // tm
    nh = DIFF_HEADS
    d = h.shape[1]
    return pl.pallas_call(
        functools.partial(_a_proj_kernel, tk=tk),
        grid=(batch * nt,),
        in_specs=[
            pl.BlockSpec((tm, d), lambda i: (i, 0)),
            _resident(g.shape), _resident(w_qkv.shape), _resident(gqk.shape), _resident(bd.shape),
        ],
        out_specs=[
            pl.BlockSpec((1, nh, tm, LANES), lambda i: (i // nt, 0, i % nt, 0)),
            pl.BlockSpec((1, nh, tm, LANES), lambda i: (i // nt, 0, i % nt, 0)),
            pl.BlockSpec((1, nh, tm // tk, VT_ROWS, tk), lambda i: (i // nt, 0, i % nt, 0, 0)),
        ],
        out_shape=[
            jax.ShapeDtypeStruct((batch, nh, seq, LANES), BF16),
            jax.ShapeDtypeStruct((batch, nh, seq, LANES), BF16),
            jax.ShapeDtypeStruct((batch, nh, seq // tk, VT_ROWS, tk), BF16),
        ],
        compiler_params=_cparams(("parallel",)),
        name="a_proj",
    )(h, g, w_qkv, gqk, bd)


def _scores(k, qz, s_ref, mb_ref):
    s = lax.dot_general(k, qz, (((1,), (1,)), ((), ())), preferred_element_type=F32)
    s_ref[...] = s
    mb_ref[...] = jnp.max(s, axis=0, keepdims=True)


def _softmax_pv(s_ref, mb_ref, extra, vt, m_sc, acc_sc):
    m_old = m_sc[...]
    if extra is None:
        s = s_ref[...]
        m_new = jnp.maximum(m_old, mb_ref[...])
    else:
        s = s_ref[...] + extra
        m_new = jnp.maximum(m_old, jnp.max(s, axis=0, keepdims=True))
    alpha = jnp.exp2(m_old - m_new)
    p = jnp.exp2(s - m_new)
    acc_sc[...] = alpha * acc_sc[...] + jnp.dot(vt, p.astype(BF16), preferred_element_type=F32)
    m_sc[...] = m_new


def _flash_init(m_sc, acc_sc):
    m_sc[...] = jnp.full(m_sc.shape, NEG_INF, F32)
    acc_sc[...] = jnp.zeros(acc_sc.shape, F32)


def _causal_pipeline(n_far, qk, sm, tails):
    qk(0, 0)

    def body(jj, carry):
        kj = 2 * jj
        qk(1, kj + 1)
        sm(0, kj, None)
        qk(0, kj + 2)
        sm(1, kj + 1, None)
        return carry

    lax.fori_loop(0, n_far // 2, body, 0)

    for cond, blocks in tails:
        @pl.when(cond)
        def _(blocks=blocks):
            for n, (kj, kind) in enumerate(blocks):
                if n + 1 < len(blocks):
                    qk(1 - n % 2, blocks[n + 1][0])
                sm(n % 2, kj, kind)


def _a_attn_kernel(q_ref, k_ref, vt_ref, bias_ref, lam_ref, sg_ref, o_ref, m_sc, acc_sc, s_sc, mb_sc,
                   *, lam_init, t, g):
    qi = pl.program_id(2)
    qz = []
    for gi in range(g):
        q = q_ref[0, gi]
        lane = lax.broadcasted_iota(jnp.int32, q.shape, 1)
        zero = jnp.zeros_like(q)
        qz.append(jnp.concatenate([jnp.where(lane < DIFF_QK_DIM, q, zero),
                                   jnp.where(lane >= DIFF_QK_DIM, q, zero)], axis=0))
    _flash_init(m_sc, acc_sc)

    def qk(slot, kj):
        for gi in range(g):
            k = k_ref[0, gi, pl.ds(pl.multiple_of(kj * t, t), t), :]
            _scores(k, qz[gi], s_sc.at[slot, gi], mb_sc.at[slot, gi])

    def sm(slot, kj, kind):
        for gi in range(g):
            extra = None if kind is None else bias_ref[gi, kind]
            _softmax_pv(s_sc.at[slot, gi], mb_sc.at[slot, gi], extra, vt_ref[0, gi, kj],
                        m_sc.at[gi], acc_sc.at[gi])

    n_far = jnp.maximum(qi - 1, 0)
    odd = n_far % 2 == 1
    _causal_pipeline(n_far, qk, sm, [
        (qi == 0, [(qi, 0)]),
        ((qi >= 1) & jnp.logical_not(odd), [(qi - 1, 1), (qi, 0)]),
        (odd, [(qi - 2, None), (qi - 1, 1), (qi, 0)]),
    ])

    lp = lam_ref[...]
    lam = (jnp.exp(jnp.sum(lp[0:1] * lp[1:2], keepdims=True))
           - jnp.exp(jnp.sum(lp[2:3] * lp[3:4], keepdims=True)) + lam_init)
    for gi in range(g):
        acc = acc_sc[gi, :DIFF_V_DIM]
        inv = 1.0 / acc_sc[gi, DIFF_V_DIM:DIFF_V_DIM + 1]
        o_t = acc[:, :t] * inv[:, :t] - lam * (acc[:, t:] * inv[:, t:])
        o = _rms_rows(o_t.T, sg_ref[...]) * (1.0 - lam_init)
        o_ref[0, :, gi * DIFF_V_DIM:(gi + 1) * DIFF_V_DIM] = o.astype(BF16)


def _a_attn(q, k, vt, bias, lam_params, sub_gain, *, lam_init):
    batch, nh, seq, _ = q.shape
    t = A_T
    g = A_G
    nq = seq // t
    return pl.pallas_call(
        functools.partial(_a_attn_kernel, lam_init=lam_init, t=t, g=g),
        grid=(batch, nh // g, nq),
        in_specs=[
            pl.BlockSpec((1, g, t, LANES), lambda b, h, i: (b, h, i, 0)),
            pl.BlockSpec((1, g, seq, LANES), lambda b, h, i: (b, h, 0, 0)),
            pl.BlockSpec((1, g, nq, VT_ROWS, t), lambda b, h, i: (b, h, 0, 0, 0)),
            pl.BlockSpec((g, 2, t, 2 * t), lambda b, h, i: (h, 0, 0, 0)),
            pl.BlockSpec(lam_params.shape, lambda b, h, i: (0, 0)),
            pl.BlockSpec(sub_gain.shape, lambda b, h, i: (0, 0)),
        ],
        out_specs=pl.BlockSpec((1, t, g * DIFF_V_DIM), lambda b, h, i: (b, i, h)),
        out_shape=jax.ShapeDtypeStruct((batch, seq, nh * DIFF_V_DIM), BF16),
        scratch_shapes=[
            pltpu.VMEM((g, 1, 2 * t), F32), pltpu.VMEM((g, VT_ROWS, 2 * t), F32),
            pltpu.VMEM((2, g, t, 2 * t), F32), pltpu.VMEM((2, g, 1, 2 * t), F32),
        ],
        compiler_params=_cparams(("parallel", "parallel", "arbitrary")),
        name="a_attn",
    )(q, k, vt, bias, lam_params, sub_gain)


def _kv_proj_kernel(h_ref, g_ref, wd_ref, cg_ref, pg_ref, pgs_ref, cos_ref, sin_ref, wu_ref, kg_ref,
                    k_ref, vt_ref, *, tk):
    tm = h_ref.shape[0]
    lora = cg_ref.shape[1]
    hn = _rms_rows(h_ref[...], g_ref[...]).astype(BF16)
    ckv = jnp.dot(hn, wd_ref[...], preferred_element_type=F32)
    c_kv = _rms_rows(ckv[:, :lora], cg_ref[...]).astype(BF16)
    pe = ckv[:, lora:lora + LANES]
    pe_sw = ckv[:, lora + LANES:]
    r = lax.rsqrt(jnp.mean(pe * pe, axis=-1, keepdims=True) + RMS_EPS)
    k_pe = ((pe * r * pg_ref[...]) * cos_ref[...] + (pe_sw * r * pgs_ref[...]) * sin_ref[...])
    lane = lax.broadcasted_iota(jnp.int32, k_pe.shape, 1)
    zero = jnp.zeros_like(k_pe)
    pe_even = jnp.where(lane < MLA_QK_ROPE, k_pe, zero).astype(BF16)
    pe_odd = jnp.where(lane >= MLA_QK_ROPE, k_pe, zero).astype(BF16)
    kv = jnp.dot(c_kv, wu_ref[...], preferred_element_type=F32)
    nope_w = MLA_HEADS * MLA_QK_NOPE
    for h in range(MLA_HEADS):
        kn = _rms_rows(kv[:, h * LANES:(h + 1) * LANES], kg_ref[...])
        k_ref[0, h, :, :LANES] = kn.astype(BF16)
        k_ref[0, h, :, LANES:] = pe_even if h % 2 == 0 else pe_odd
        v_h = kv[:, nope_w + h * LANES:nope_w + (h + 1) * LANES]
        for c in range(tm // tk):
            _store_vt(vt_ref.at[0, h, c], v_h[c * tk:(c + 1) * tk])


def _kv_proj(h, g, wd, cg, pg, pgs, cos_t, sin_t, wu, kg, *, batch, seq, tk):
    tm = PROJ_TM
    nt = seq // tm
    nh = MLA_HEADS
    d = h.shape[1]
    return pl.pallas_call(
        functools.partial(_kv_proj_kernel, tk=tk),
        grid=(batch * nt,),
        in_specs=[
            pl.BlockSpec((tm, d), lambda i: (i, 0)),
            _resident(g.shape), _resident(wd.shape), _resident(cg.shape),
            _resident(pg.shape), _resident(pgs.shape),
            pl.BlockSpec((tm, LANES), lambda i: (i, 0)),
            pl.BlockSpec((tm, LANES), lambda i: (i, 0)),
            _resident(wu.shape), _resident(kg.shape),
        ],
        out_specs=[
            pl.BlockSpec((1, nh, tm, 2 * LANES), lambda i: (i // nt, 0, i % nt, 0)),
            pl.BlockSpec((1, nh, tm // tk, VT_ROWS, tk), lambda i: (i // nt, 0, i % nt, 0, 0)),
        ],
        out_shape=[
            jax.ShapeDtypeStruct((batch, nh, seq, 2 * LANES), BF16),
            jax.ShapeDtypeStruct((batch, nh, seq // tk, VT_ROWS, tk), BF16),
        ],
        compiler_params=_cparams(("parallel",)),
        name="kv_proj",
    )(h, g, wd, cg, pg, pgs, cos_t, sin_t, wu, kg)


def _b_proj_kernel(h_ref, g_ref, wdq_ref, cqg_ref, wuq_ref, gn_ref, gp_ref, gps_ref, cos_ref, sin_ref,
                   bd_ref, q_ref):
    nope_w = MLA_HEADS * MLA_QK_NOPE
    pe_w = MLA_HEADS * MLA_QK_ROPE
    hn = _rms_rows(h_ref[...], g_ref[...]).astype(BF16)
    cq = jnp.dot(hn, wdq_ref[...], preferred_element_type=F32)
    cqn = _rms_rows(cq, cqg_ref[...]).astype(BF16)
    q = jnp.dot(cqn, wuq_ref[...], preferred_element_type=F32)
    qp = q[:, nope_w:nope_w + pe_w]
    qps = q[:, nope_w + pe_w:]
    r = lax.rsqrt(_group_mean_sq(qp, bd_ref[...], MLA_QK_ROPE) + RMS_EPS)
    reps = pe_w // LANES
    cos_t = jnp.concatenate([cos_ref[...]] * reps, axis=1)
    sin_t = jnp.concatenate([sin_ref[...]] * reps, axis=1)
    rot = (qp * r * gp_ref[...]) * cos_t + (qps * r * gps_ref[...]) * sin_t
    lane = lax.broadcasted_iota(jnp.int32, (q.shape[0], LANES), 1)
    zero = jnp.zeros((q.shape[0], LANES), F32)
    for h in range(MLA_HEADS):
        qn = _rms_rows(q[:, h * LANES:(h + 1) * LANES], gn_ref[...])
        pair = rot[:, (h // 2) * LANES:(h // 2 + 1) * LANES]
        keep = (lane < MLA_QK_ROPE) if h % 2 == 0 else (lane >= MLA_QK_ROPE)
        q_ref[:, h * 2 * LANES:h * 2 * LANES + LANES] = qn.astype(BF16)
        q_ref[:, h * 2 * LANES + LANES:(h + 1) * 2 * LANES] = jnp.where(keep, pair, zero).astype(BF16)


def _b_proj(h, g, wdq, cqg, wuq, gn, gp, gps, cos_t, sin_t, bd):
    t, d = h.shape
    tm = PROJ_TM
    qw = MLA_HEADS * 2 * LANES
    return pl.pallas_call(
        _b_proj_kernel,
        grid=(t // tm,),
        in_specs=[
            pl.BlockSpec((tm, d), lambda i: (i, 0)),
            _resident(g.shape), _resident(wdq.shape), _resident(cqg.shape), _resident(wuq.shape),
            _resident(gn.shape), _resident(gp.shape), _resident(gps.shape),
            pl.BlockSpec((tm, LANES), lambda i: (i, 0)),
            pl.BlockSpec((tm, LANES), lambda i: (i, 0)),
            _resident(bd.shape),
        ],
        out_specs=pl.BlockSpec((tm, qw), lambda i: (i, 0)),
        out_shape=jax.ShapeDtypeStruct((t, qw), BF16),
        compiler_params=_cparams(("parallel",)),
        name="b_proj",
    )(h, g, wdq, cqg, wuq, gn, gp, gps, cos_t, sin_t, bd)


def _b_attn_kernel(q_ref, k_ref, vt_ref, mask_ref, o_ref, m_sc, acc_sc, s_sc, mb_sc, *, t, g):
    qi = pl.program_id(2)
    qw = 2 * LANES
    qs = [q_ref[:, gi * qw:(gi + 1) * qw] for gi in range(g)]
    _flash_init(m_sc, acc_sc)

    def qk(slot, kj):
        for gi in range(g):
            k = k_ref[0, gi, pl.ds(pl.multiple_of(kj * t, t), t), :]
            _scores(k, qs[gi], s_sc.at[slot, gi], mb_sc.at[slot, gi])

    def sm(slot, kj, kind):
        for gi in range(g):
            extra = None if kind is None else mask_ref[...]
            _softmax_pv(s_sc.at[slot, gi], mb_sc.at[slot, gi], extra, vt_ref[0, gi, kj],
                        m_sc.at[gi], acc_sc.at[gi])

    odd = qi % 2 == 1
    _causal_pipeline(qi, qk, sm, [
        (jnp.logical_not(odd), [(qi, 0)]),
        (odd, [(qi - 1, None), (qi, 0)]),
    ])

    for gi in range(g):
        o_t = acc_sc[gi, :MLA_V_DIM] * (1.0 / acc_sc[gi, MLA_V_DIM:MLA_V_DIM + 1])
        o_ref[:, gi * MLA_V_DIM:(gi + 1) * MLA_V_DIM] = o_t.T.astype(BF16)


def _b_attn(q, k, vt, mask, *, batch, seq):
    nh = MLA_HEADS
    t = B_T
    g = B_G
    nq = seq // t
    return pl.pallas_call(
        functools.partial(_b_attn_kernel, t=t, g=g),
        grid=(batch, nh // g, nq),
        in_specs=[
            pl.BlockSpec((t, g * 2 * LANES), lambda b, h, i: (b * nq + i, h)),
            pl.BlockSpec((1, g, seq, 2 * LANES), lambda b, h, i: (b, h, 0, 0)),
            pl.BlockSpec((1, g, nq, VT_ROWS, t), lambda b, h, i: (b, h, 0, 0, 0)),
            pl.BlockSpec(mask.shape, lambda b, h, i: (0, 0)),
        ],
        out_specs=pl.BlockSpec((t, g * MLA_V_DIM), lambda b, h, i: (b * nq + i, h)),
        out_shape=jax.ShapeDtypeStruct((batch * seq, nh * MLA_V_DIM), BF16),
        scratch_shapes=[
            pltpu.VMEM((g, 1, t), F32), pltpu.VMEM((g, VT_ROWS, t), F32),
            pltpu.VMEM((2, g, t, t), F32), pltpu.VMEM((2, g, 1, t), F32),
        ],
        compiler_params=_cparams(("parallel", "parallel", "arbitrary")),
        name="b_attn",
    )(q, k, vt, mask)


def _post_kernel(h_ref, o_ref, wo_ref, fg_ref, win_ref, cw_ref, cb_ref, wout_ref, pg_ref, wg_ref,
                 p_ref, wp_ref, out_ref, carry_sc, *, tiles_per_batch, d_ff):
    tm = h_ref.shape[0]
    ts = tm // 2
    halves = (slice(0, ts), slice(ts, tm))

    @pl.when((pl.program_id(0) % tiles_per_batch) == 0)
    def _():
        carry_sc[...] = jnp.zeros(carry_sc.shape, F32)

    h1 = [h_ref[r, :] + jnp.dot(o_ref[r, :], wo_ref[...], preferred_element_type=F32) for r in halves]
    hn = [_rms_rows(x, fg_ref[...]).astype(BF16) for x in h1]

    def up(lo, width):
        return [[jnp.dot(x, win_ref[:, c:c + width], preferred_element_type=F32)
                 for c in (lo, d_ff + lo)] for x in hn]

    def conv(u_halves, lo, width):
        prev = carry_sc[:, lo:lo + width]
        cw = cw_ref[:, lo:lo + width]
        outs = []
        for u in u_halves:
            ext = jnp.concatenate([prev, u], axis=0)
            outs.append(cb_ref[:, lo:lo + width] + ext[6:6 + ts] * cw[0:1] + ext[7:7 + ts] * cw[1:2]
                        + u * cw[2:3])
            prev = u[ts - 8:, :]
        carry_sc[:, lo:lo + width] = prev
        return outs

    chunks = [(lo, min(FF_CHUNK, d_ff - lo)) for lo in range(0, d_ff, FF_CHUNK)]
    u_next = up(*chunks[0])
    acc = [None, None]
    for ci, (lo, width) in enumerate(chunks):
        u = u_next
        if ci + 1 < len(chunks):
            u_next = up(*chunks[ci + 1])
        a = conv([u[0][0], u[1][0]], lo, width)
        gt = conv([u[0][1], u[1][1]], d_ff + lo, width)
        for i in range(2):
            act = (gt[i] * jax.nn.sigmoid(gt[i]) * a[i]).astype(BF16)
            part = jnp.dot(act, wout_ref[lo:lo + width, :], preferred_element_type=F32)
            acc[i] = part if acc[i] is None else acc[i] + part
    pe = [jnp.dot(p_ref[r, :].astype(BF16), wp_ref[...], preferred_element_type=F32) for r in halves]
    for i, r in enumerate(halves):
        h2 = h1[i] + acc[i]
        gate = jax.nn.sigmoid(jnp.dot(_rms_rows(h2, pg_ref[...]).astype(BF16), wg_ref[...],
                                      preferred_element_type=F32))
        out_ref[r, :] = h2 + pe[i] * gate


def _post(h, o, wo, fg, win, cw, cb, wout, pg, wg, p_i, wp, *, seq):
    t, d = h.shape
    tm = POST_TM
    d_ff = wout.shape[0]
    return pl.pallas_call(
        functools.partial(_post_kernel, tiles_per_batch=seq // tm, d_ff=d_ff),
        grid=(t // tm,),
        in_specs=[
            pl.BlockSpec((tm, d), lambda i: (i, 0)),
            pl.BlockSpec((tm, o.shape[1]), lambda i: (i, 0)),
            _resident(wo.shape), _resident(fg.shape), _resident(win.shape), _resident(cw.shape),
            _resident(cb.shape), _resident(wout.shape), _resident(pg.shape), _resident(wg.shape),
            pl.BlockSpec((tm, p_i.shape[1]), lambda i: (i, 0)),
            _resident(wp.shape),
        ],
        out_specs=pl.BlockSpec((tm, d), lambda i: (i, 0)),
        out_shape=jax.ShapeDtypeStruct((t, d), F32),
        scratch_shapes=[pltpu.VMEM((8, 2 * d_ff), F32)],
        compiler_params=_cparams(("arbitrary",)),
        name="post",
    )(h, o, wo, fg, win, cw, cb, wout, pg, wg, p_i, wp)


def _t5_bucket_table():
    n = np.arange(T5_MAX_DISTANCE + 1)
    max_exact = T5_BUCKETS // 2
    log_ratio = np.log(np.maximum(n, 1) / max_exact) / math.log(T5_MAX_DISTANCE / max_exact)
    large = np.minimum(max_exact + (log_ratio * (T5_BUCKETS - max_exact)).astype(np.int32),
                       T5_BUCKETS - 1)
    return np.where(n < max_exact, n, large).astype(np.int32)


def _bias_tiles(rel_table, t):
    assert t >= T5_MAX_DISTANCE
    nh = rel_table.shape[1]
    rel = rel_table[_t5_bucket_table()]
    rel = (rel - rel[T5_MAX_DISTANCE]).astype(F32)
    near = rel[:T5_MAX_DISTANCE]
    zeros = jnp.zeros((t - T5_MAX_DISTANCE, nh), F32)

    def toeplitz(v_pos, v_neg):
        v = jnp.concatenate([v_pos, jnp.zeros((1, nh), F32), v_neg], axis=0)
        u = jnp.tile(v, (t, 1))[:t * (2 * t - 1)]
        return u.reshape(t, 2 * t - 1, nh)[:, :t]

    b0 = toeplitz(jnp.concatenate([near, zeros], axis=0), jnp.full((t - 1, nh), NEG_INF, F32))
    b1 = toeplitz(jnp.zeros((t, nh), F32), jnp.concatenate([near[1:], zeros], axis=0))
    tiles = jnp.stack([b0, b1], axis=0).transpose(3, 0, 1, 2)
    return jnp.concatenate([tiles, tiles], axis=-1)


def _causal_tile(t):
    kk = np.arange(t)[:, None]
    qq = np.arange(t)[None, :]
    return jnp.asarray(np.where(qq >= kk, 0.0, NEG_INF), F32)


def _block_diag_ones(group):
    idx = np.arange(MXU_DIM) // group
    return jnp.asarray(idx[:, None] == idx[None, :], BF16)


def _swap_halves(v):
    half = v.shape[-1] // 2
    return jnp.concatenate([v[..., half:], v[..., :half]], axis=-1)


def kernel(x, p, positions, rel_bias_table, attn_norm, a_w_qkv, a_q_norm, a_k_norm, a_lam_q1, a_lam_k1, a_lam_q2, a_lam_k2, a_sub_norm, a_w_o, kv_norm, w_dkv, ckv_norm, w_ukv, k_nope_norm, k_pe_norm, b_w_dq, b_cq_norm, b_w_uq, b_q_nope_norm, b_q_pe_norm, b_w_o, ffn_norm, ffn_w_in, ffn_conv_w, ffn_conv_b, ffn_w_out, ple_norm, ple_w_proj, ple_w_gate):
    batch, seq, d = x.shape
    depth = p.shape[0]
    n_a = a_w_qkv.shape[0]
    t = batch * seq
    row = lambda v: v.reshape(1, -1).astype(F32)

    h = x.reshape(t, d)
    bd64 = _block_diag_ones(DIFF_QK_DIM)
    bias = _bias_tiles(rel_bias_table, A_T) * LOG2E
    cos_t, sin_t = _rope_tables(positions.reshape(t, 1))
    shared = None

    for i in range(depth):
        if i < n_a:
            lam_init = 0.8 - 0.6 * math.exp(-0.3 * i)
            q_scale = DIFF_QK_DIM ** -0.5 * LOG2E
            gqk = jnp.concatenate([jnp.tile(a_q_norm[i], 2 * DIFF_HEADS) * q_scale,
                                   jnp.tile(a_k_norm[i], 2 * DIFF_HEADS)]).reshape(1, -1)
            q, k, vt = _a_proj(h, row(attn_norm[i]), a_w_qkv[i].astype(BF16), gqk, bd64,
                               batch=batch, seq=seq, tk=A_T)
            lam_params = jnp.stack([a_lam_q1[i], a_lam_k1[i], a_lam_q2[i], a_lam_k2[i]])
            o = _a_attn(q, k, vt, bias, lam_params, row(a_sub_norm[i]), lam_init=lam_init)
            o = o.reshape(t, -1)
            w_o = a_w_o[i]
        else:
            j = i - n_a
            if shared is None:
                lora = ckv_norm.shape[0]
                w_pe = w_dkv[:, lora:]
                w_pe_sw = _swap_halves(w_pe)
                wd = jnp.concatenate([w_dkv[:, :lora], w_pe, w_pe, w_pe_sw, w_pe_sw], axis=1)
                wu = w_ukv.reshape(lora, MLA_HEADS, MLA_QK_NOPE + MLA_V_DIM)
                wu = jnp.concatenate([wu[:, :, :MLA_QK_NOPE].reshape(lora, -1),
                                      wu[:, :, MLA_QK_NOPE:].reshape(lora, -1)], axis=1)
                shared = _kv_proj(h, row(kv_norm), wd.astype(BF16), row(ckv_norm),
                                  row(jnp.tile(k_pe_norm, 2)), row(jnp.tile(_swap_halves(k_pe_norm), 2)),
                                  cos_t, sin_t, wu.astype(BF16), row(k_nope_norm),
                                  batch=batch, seq=seq, tk=B_T)
                mask = _causal_tile(B_T)
            q_lora = b_w_uq.shape[1]
            wq = b_w_uq[j].reshape(q_lora, MLA_HEADS, MLA_QK_NOPE + MLA_QK_ROPE)
            wq_pe = wq[:, :, MLA_QK_NOPE:]
            wuq = jnp.concatenate([wq[:, :, :MLA_QK_NOPE].reshape(q_lora, -1),
                                   wq_pe.reshape(q_lora, -1),
                                   _swap_halves(wq_pe).reshape(q_lora, -1)], axis=1)
            scale = (MLA_QK_NOPE + MLA_QK_ROPE) ** -0.5 * LOG2E
            gp = b_q_pe_norm[j] * scale
            qf = _b_proj(h, row(attn_norm[i]), b_w_dq[j].astype(BF16), row(b_cq_norm[j]),
                         wuq.astype(BF16), row(b_q_nope_norm[j] * scale),
                         row(jnp.tile(gp, MLA_HEADS)), row(jnp.tile(_swap_halves(gp), MLA_HEADS)),
                         cos_t, sin_t, bd64)
            o = _b_attn(qf, shared[0], shared[1], mask, batch=batch, seq=seq)
            w_o = b_w_o[j]
        h = _post(h, o, w_o.astype(BF16), row(ffn_norm[i]), ffn_w_in[i].astype(BF16),
                  ffn_conv_w[i], row(ffn_conv_b[i]), ffn_w_out[i].astype(BF16), row(ple_norm[i]),
                  ple_w_gate[i].astype(BF16), p[i].reshape(t, -1), ple_w_proj[i].astype(BF16),
                  seq=seq)
    return h.reshape(batch, seq, d)
```

```python
import functools
import math

import numpy as np
import jax
import jax.numpy as jnp
from jax import lax
from jax.experimental import pallas as pl
from jax.experimental.pallas import tpu as pltpu

F32 = jnp.float32
BF16 = jnp.bfloat16

DIFF_HEADS = 8
DIFF_QK_DIM = 64
DIFF_V_DIM = 128
T5_BUCKETS = 32
T5_MAX_DISTANCE = 128
MLA_HEADS = 16
MLA_QK_NOPE = 128
MLA_QK_ROPE = 64
MLA_V_DIM = 128
ROPE_BASE = 10000.0
RMS_EPS = 1e-6
NEG_INF = -1e30
LOG2E = math.log2(math.e)

LANES = 128
MXU_DIM = 256
BF16_SUBLANES = 16
VT_ROWS = DIFF_V_DIM + BF16_SUBLANES
VMEM_LIMIT_BYTES = 56 * 1024 * 1024

PROJ_TM = 512
POST_TM = 512
A_T = 256
A_G = 4
B_T = 512
B_G = 4
FF_CHUNK = 768


def _cparams(sem):
    return pltpu.CompilerParams(dimension_semantics=sem, vmem_limit_bytes=VMEM_LIMIT_BYTES)


def _resident(shape):
    nd = len(shape)
    return pl.BlockSpec(shape, lambda *_: (0,) * nd, pipeline_mode=pl.Buffered(1))


def _rms_rows(x, g):
    ms = jnp.mean(x * x, axis=-1, keepdims=True)
    return x * lax.rsqrt(ms + RMS_EPS) * g


def _store_vt(vt_ref, v):
    vdim = v.shape[1]
    vt_ref[:vdim, :] = v.T.astype(BF16)
    vt_ref[vdim:, :] = jnp.ones((VT_ROWS - vdim, v.shape[0]), BF16)


def _group_mean_sq(x, bd, group):
    y = (x * x).astype(BF16)
    parts = [jnp.dot(y[:, c:c + MXU_DIM], bd, preferred_element_type=F32)
             for c in range(0, x.shape[1], MXU_DIM)]
    return jnp.concatenate(parts, axis=1) * (1.0 / group)


def _rope_kernel(pos_ref, cos_ref, sin_ref):
    pos = pos_ref[...].astype(F32)
    lane = lax.broadcasted_iota(jnp.int32, (1, LANES), 1)
    half = MLA_QK_ROPE // 2
    r = (lane % half).astype(F32)
    inv_freq = jnp.exp(-math.log(ROPE_BASE) * r * (2.0 / MLA_QK_ROPE))
    ang = pos * inv_freq
    cos_ref[...] = jnp.cos(ang)
    sin_ref[...] = jnp.where((lane % MLA_QK_ROPE) < half, -jnp.sin(ang), jnp.sin(ang))


def _rope_tables(pos_col):
    t = pos_col.shape[0]
    tm = 2048
    return pl.pallas_call(
        _rope_kernel,
        grid=(t // tm,),
        in_specs=[pl.BlockSpec((tm, 1), lambda i: (i, 0))],
        out_specs=[pl.BlockSpec((tm, LANES), lambda i: (i, 0))] * 2,
        out_shape=[jax.ShapeDtypeStruct((t, LANES), F32)] * 2,
        compiler_params=_cparams(("parallel",)),
        name="rope_tables",
    )(pos_col)


def _a_proj_kernel(h_ref, g_ref, w_ref, gqk_ref, bd_ref, q_ref, k_ref, vt_ref, *, tk):
    tm = h_ref.shape[0]
    qk_w = DIFF_HEADS * 2 * DIFF_QK_DIM
    hn = _rms_rows(h_ref[...], g_ref[...]).astype(BF16)
    qkv = jnp.dot(hn, w_ref[...], preferred_element_type=F32)
    qk = qkv[:, :2 * qk_w]
    ms = _group_mean_sq(qk, bd_ref[...], DIFF_QK_DIM)
    qkn = qk * lax.rsqrt(ms + RMS_EPS) * gqk_ref[...]
    for h in range(DIFF_HEADS):
        q_ref[0, h] = qkn[:, h * LANES:(h + 1) * LANES].astype(BF16)
        k_ref[0, h] = qkn[:, qk_w + h * LANES:qk_w + (h + 1) * LANES].astype(BF16)
        v_h = qkv[:, 2 * qk_w + h * LANES:2 * qk_w + (h + 1) * LANES]
        for c in range(tm // tk):
            _store_vt(vt_ref.at[0, h, c], v_h[c * tk:(c + 1) * tk])


def _a_proj(h, g, w_qkv, gqk, bd, *, batch, seq, tk):
    tm = PROJ_TM
    nt = seq // tm
    nh = DIFF_HEADS
    d = h.shape[1]
    return pl.pallas_call(
        functools.partial(_a_proj_kernel, tk=tk),
        grid=(batch * nt,),
        in_specs=[
            pl.BlockSpec((tm, d), lambda i: (i, 0)),
            _resident(g.shape), _resident(w_qkv.shape), _resident(gqk.shape), _resident(bd.shape),
        ],
        out_specs=[
            pl.BlockSpec((1, nh, tm, LANES), lambda i: (i // nt, 0, i % nt, 0)),
            pl.BlockSpec((1, nh, tm, LANES), lambda i: (i // nt, 0, i % nt, 0)),
            pl.BlockSpec((1, nh, tm // tk, VT_ROWS, tk), lambda i: (i // nt, 0, i % nt, 0, 0)),
        ],
        out_shape=[
            jax.ShapeDtypeStruct((batch, nh, seq, LANES), BF16),
            jax.ShapeDtypeStruct((batch, nh, seq, LANES), BF16),
            jax.ShapeDtypeStruct((batch, nh, seq // tk, VT_ROWS, tk), BF16),
        ],
        compiler_params=_cparams(("parallel",)),
        name="a_proj",
    )(h, g, w_qkv, gqk, bd)


def _scores(k, qz, s_ref, mb_ref):
    s = lax.dot_general(k, qz, (((1,), (1,)), ((), ())), preferred_element_type=F32)
    s_ref[...] = s
    mb_ref[...] = jnp.max(s, axis=0, keepdims=True)


def _softmax_pv(s_ref, mb_ref, extra, vt, m_sc, acc_sc):
    m_old = m_sc[...]
    if extra is None:
        s = s_ref[...]
        m_new = jnp.maximum(m_old, mb_ref[...])
    else:
        s = s_ref[...] + extra
        m_new = jnp.maximum(m_old, jnp.max(s, axis=0, keepdims=True))
    alpha = jnp.exp2(m_old - m_new)
    p = jnp.exp2(s - m_new)
    acc_sc[...] = alpha * acc_sc[...] + jnp.dot(vt, p.astype(BF16), preferred_element_type=F32)
    m_sc[...] = m_new


def _flash_init(m_sc, acc_sc):
    m_sc[...] = jnp.full(m_sc.shape, NEG_INF, F32)
    acc_sc[...] = jnp.zeros(acc_sc.shape, F32)


def _causal_pipeline(n_far, qk, sm, tails):
    def body(jj, carry):
        kj = 2 * jj
        qk(1, kj + 1)
        sm(0, kj, None)
        qk(0, kj + 2)
        sm(1, kj + 1, None)
        return carry

    lax.fori_loop(0, n_far // 2, body, 0)

    for cond, blocks in tails:
        @pl.when(cond)
        def _(blocks=blocks):
            for n, (kj, kind) in enumerate(blocks):
                if n + 1 < len(blocks):
                    qk(1 - n % 2, blocks[n + 1][0])
                sm(n % 2, kj, kind)


def _a_attn_kernel(q_ref, qn_ref, k_ref, vt_ref, bias_ref, lam_ref, sg_ref, o_ref, m_sc, acc_sc, s_sc,
                   mb_sc, *, lam_init, t, g):
    qi = pl.program_id(2)

    def stacked(ref):
        out = []
        for gi in range(g):
            q = ref[0, gi]
            lane = lax.broadcasted_iota(jnp.int32, q.shape, 1)
            zero = jnp.zeros_like(q)
            out.append(jnp.concatenate([jnp.where(lane < DIFF_QK_DIM, q, zero),
                                        jnp.where(lane >= DIFF_QK_DIM, q, zero)], axis=0))
        return out

    qz = stacked(q_ref)
    _flash_init(m_sc, acc_sc)

    def qk(slot, kj, qs=qz):
        for gi in range(g):
            k = k_ref[0, gi, pl.ds(pl.multiple_of(kj * t, t), t), :]
            _scores(k, qs[gi], s_sc.at[slot, gi], mb_sc.at[slot, gi])

    @pl.when(qi == 0)
    def _():
        qk(0, 0)

    def sm(slot, kj, kind):
        for gi in range(g):
            extra = None if kind is None else bias_ref[gi, kind]
            _softmax_pv(s_sc.at[slot, gi], mb_sc.at[slot, gi], extra, vt_ref[0, gi, kj],
                        m_sc.at[gi], acc_sc.at[gi])

    n_far = jnp.maximum(qi - 1, 0)
    odd = n_far % 2 == 1
    _causal_pipeline(n_far, qk, sm, [
        (qi == 0, [(qi, 0)]),
        ((qi >= 1) & jnp.logical_not(odd), [(qi - 1, 1), (qi, 0)]),
        (odd, [(qi - 2, None), (qi - 1, 1), (qi, 0)]),
    ])

    qk(0, 0, stacked(qn_ref))

    lp = lam_ref[...]
    lam = (jnp.exp(jnp.sum(lp[0:1] * lp[1:2], keepdims=True))
           - jnp.exp(jnp.sum(lp[2:3] * lp[3:4], keepdims=True)) + lam_init)
    for gi in range(g):
        acc = acc_sc[gi, :DIFF_V_DIM]
        inv = 1.0 / acc_sc[gi, DIFF_V_DIM:DIFF_V_DIM + 1]
        o_t = acc[:, :t] * inv[:, :t] - lam * (acc[:, t:] * inv[:, t:])
        o = _rms_rows(o_t.T, sg_ref[...]) * (1.0 - lam_init)
        o_ref[0, :, gi * DIFF_V_DIM:(gi + 1) * DIFF_V_DIM] = o.astype(BF16)


def _a_attn(q, k, vt, bias, lam_params, sub_gain, *, lam_init):
    batch, nh, seq, _ = q.shape
    t = A_T
    g = A_G
    nq = seq // t
    return pl.pallas_call(
        functools.partial(_a_attn_kernel, lam_init=lam_init, t=t, g=g),
        grid=(batch, nh // g, nq),
        in_specs=[
            pl.BlockSpec((1, g, t, LANES), lambda b, h, i: (b, h, i, 0)),
            pl.BlockSpec((1, g, t, LANES), lambda b, h, i: (b, h, jnp.minimum(i + 1, nq - 1), 0)),
            pl.BlockSpec((1, g, seq, LANES), lambda b, h, i: (b, h, 0, 0)),
            pl.BlockSpec((1, g, nq, VT_ROWS, t), lambda b, h, i: (b, h, 0, 0, 0)),
            pl.BlockSpec((g, 2, t, 2 * t), lambda b, h, i: (h, 0, 0, 0)),
            pl.BlockSpec(lam_params.shape, lambda b, h, i: (0, 0)),
            pl.BlockSpec(sub_gain.shape, lambda b, h, i: (0, 0)),
        ],
        out_specs=pl.BlockSpec((1, t, g * DIFF_V_DIM), lambda b, h, i: (b, i, h)),
        out_shape=jax.ShapeDtypeStruct((batch, seq, nh * DIFF_V_DIM), BF16),
        scratch_shapes=[
            pltpu.VMEM((g, 1, 2 * t), F32), pltpu.VMEM((g, VT_ROWS, 2 * t), F32),
            pltpu.VMEM((2, g, t, 2 * t), F32), pltpu.VMEM((2, g, 1, 2 * t), F32),
        ],
        compiler_params=_cparams(("arbitrary", "arbitrary", "arbitrary")),
        name="a_attn",
    )(q, q, k, vt, bias, lam_params, sub_gain)


def _kv_proj_kernel(h_ref, g_ref, wd_ref, cg_ref, pg_ref, pgs_ref, cos_ref, sin_ref, wu_ref, kg_ref,
                    k_ref, vt_ref, *, tk):
    tm = h_ref.shape[0]
    lora = cg_ref.shape[1]
    hn = _rms_rows(h_ref[...], g_ref[...]).astype(BF16)
    ckv = jnp.dot(hn, wd_ref[...], preferred_element_type=F32)
    c_kv = _rms_rows(ckv[:, :lora], cg_ref[...]).astype(BF16)
    pe = ckv[:, lora:lora + LANES]
    pe_sw = ckv[:, lora + LANES:]
    r = lax.rsqrt(jnp.mean(pe * pe, axis=-1, keepdims=True) + RMS_EPS)
    k_pe = ((pe * r * pg_ref[...]) * cos_ref[...] + (pe_sw * r * pgs_ref[...]) * sin_ref[...])
    lane = lax.broadcasted_iota(jnp.int32, k_pe.shape, 1)
    zero = jnp.zeros_like(k_pe)
    pe_even = jnp.where(lane < MLA_QK_ROPE, k_pe, zero).astype(BF16)
    pe_odd = jnp.where(lane >= MLA_QK_ROPE, k_pe, zero).astype(BF16)
    kv = jnp.dot(c_kv, wu_ref[...], preferred_element_type=F32)
    nope_w = MLA_HEADS * MLA_QK_NOPE
    for h in range(MLA_HEADS):
        kn = _rms_rows(kv[:, h * LANES:(h + 1) * LANES], kg_ref[...])
        k_ref[0, h, :, :LANES] = kn.astype(BF16)
        k_ref[0, h, :, LANES:] = pe_even if h % 2 == 0 else pe_odd
        v_h = kv[:, nope_w + h * LANES:nope_w + (h + 1) * LANES]
        for c in range(tm // tk):
            _store_vt(vt_ref.at[0, h, c], v_h[c * tk:(c + 1) * tk])


def _kv_proj(h, g, wd, cg, pg, pgs, cos_t, sin_t, wu, kg, *, batch, seq, tk):
    tm = PROJ_TM
    nt = seq // tm
    nh = MLA_HEADS
    d = h.shape[1]
    return pl.pallas_call(
        functools.partial(_kv_proj_kernel, tk=tk),
        grid=(batch * nt,),
        in_specs=[
            pl.BlockSpec((tm, d), lambda i: (i, 0)),
            _resident(g.shape), _resident(wd.shape), _resident(cg.shape),
            _resident(pg.shape), _resident(pgs.shape),
            pl.BlockSpec((tm, LANES), lambda i: (i, 0)),
            pl.BlockSpec((tm, LANES), lambda i: (i, 0)),
            _resident(wu.shape), _resident(kg.shape),
        ],
        out_specs=[
            pl.BlockSpec((1, nh, tm, 2 * LANES), lambda i: (i // nt, 0, i % nt, 0)),
            pl.BlockSpec((1, nh, tm // tk, VT_ROWS, tk), lambda i: (i // nt, 0, i % nt, 0, 0)),
        ],
        out_shape=[
            jax.ShapeDtypeStruct((batch, nh, seq, 2 * LANES), BF16),
            jax.ShapeDtypeStruct((batch, nh, seq // tk, VT_ROWS, tk), BF16),
        ],
        compiler_params=_cparams(("parallel",)),
        name="kv_proj",
    )(h, g, wd, cg, pg, pgs, cos_t, sin_t, wu, kg)


def _b_proj_kernel(h_ref, g_ref, wdq_ref, cqg_ref, wuq_ref, gn_ref, gp_ref, gps_ref, cos_ref, sin_ref,
                   bd_ref, q_ref):
    nope_w = MLA_HEADS * MLA_QK_NOPE
    pe_w = MLA_HEADS * MLA_QK_ROPE
    hn = _rms_rows(h_ref[...], g_ref[...]).astype(BF16)
    cq = jnp.dot(hn, wdq_ref[...], preferred_element_type=F32)
    cqn = _rms_rows(cq, cqg_ref[...]).astype(BF16)
    q = jnp.dot(cqn, wuq_ref[...], preferred_element_type=F32)
    qp = q[:, nope_w:nope_w + pe_w]
    qps = q[:, nope_w + pe_w:]
    r = lax.rsqrt(_group_mean_sq(qp, bd_ref[...], MLA_QK_ROPE) + RMS_EPS)
    reps = pe_w // LANES
    cos_t = jnp.concatenate([cos_ref[...]] * reps, axis=1)
    sin_t = jnp.concatenate([sin_ref[...]] * reps, axis=1)
    rot = (qp * r * gp_ref[...]) * cos_t + (qps * r * gps_ref[...]) * sin_t
    lane = lax.broadcasted_iota(jnp.int32, (q.shape[0], LANES), 1)
    zero = jnp.zeros((q.shape[0], LANES), F32)
    for h in range(MLA_HEADS):
        qn = _rms_rows(q[:, h * LANES:(h + 1) * LANES], gn_ref[...])
        pair = rot[:, (h // 2) * LANES:(h // 2 + 1) * LANES]
        keep = (lane < MLA_QK_ROPE) if h % 2 == 0 else (lane >= MLA_QK_ROPE)
        q_ref[:, h * 2 * LANES:h * 2 * LANES + LANES] = qn.astype(BF16)
        q_ref[:, h * 2 * LANES + LANES:(h + 1) * 2 * LANES] = jnp.where(keep, pair, zero).astype(BF16)


def _b_proj(h, g, wdq, cqg, wuq, gn, gp, gps, cos_t, sin_t, bd):
    t, d = h.shape
    tm = PROJ_TM
    qw = MLA_HEADS * 2 * LANES
    return pl.pallas_call(
        _b_proj_kernel,
        grid=(t // tm,),
        in_specs=[
            pl.BlockSpec((tm, d), lambda i: (i, 0)),
            _resident(g.shape), _resident(wdq.shape), _resident(cqg.shape), _resident(wuq.shape),
            _resident(gn.shape), _resident(gp.shape), _resident(gps.shape),
            pl.BlockSpec((tm, LANES), lambda i: (i, 0)),
            pl.BlockSpec((tm, LANES), lambda i: (i, 0)),
            _resident(bd.shape),
        ],
        out_specs=pl.BlockSpec((tm, qw), lambda i: (i, 0)),
        out_shape=jax.ShapeDtypeStruct((t, qw), BF16),
        compiler_params=_cparams(("parallel",)),
        name="b_proj",
    )(h, g, wdq, cqg, wuq, gn, gp, gps, cos_t, sin_t, bd)


def _b_attn_kernel(q_ref, qn_ref, k_ref, vt_ref, mask_ref, o_ref, m_sc, acc_sc, s_sc, mb_sc, *, t, g):
    qi = pl.program_id(2)
    qw = 2 * LANES
    _flash_init(m_sc, acc_sc)

    def qk(slot, kj, ref=q_ref):
        for gi in range(g):
            k = k_ref[0, gi, pl.ds(pl.multiple_of(kj * t, t), t), :]
            _scores(k, ref[:, gi * qw:(gi + 1) * qw], s_sc.at[slot, gi], mb_sc.at[slot, gi])

    @pl.when(qi == 0)
    def _():
        qk(0, 0)

    def sm(slot, kj, kind):
        for gi in range(g):
            extra = None if kind is None else mask_ref[...]
            _softmax_pv(s_sc.at[slot, gi], mb_sc.at[slot, gi], extra, vt_ref[0, gi, kj],
                        m_sc.at[gi], acc_sc.at[gi])

    odd = qi % 2 == 1
    _causal_pipeline(qi, qk, sm, [
        (jnp.logical_not(odd), [(qi, 0)]),
        (odd, [(qi - 1, None), (qi, 0)]),
    ])

    qk(0, 0, qn_ref)

    for gi in range(g):
        o_t = acc_sc[gi, :MLA_V_DIM] * (1.0 / acc_sc[gi, MLA_V_DIM:MLA_V_DIM + 1])
        o_ref[:, gi * MLA_V_DIM:(gi + 1) * MLA_V_DIM] = o_t.T.astype(BF16)


def _b_attn(q, k, vt, mask, *, batch, seq):
    nh = MLA_HEADS
    t = B_T
    g = B_G
    nq = seq // t
    return pl.pallas_call(
        functools.partial(_b_attn_kernel, t=t, g=g),
        grid=(batch, nh // g, nq),
        in_specs=[
            pl.BlockSpec((t, g * 2 * LANES), lambda b, h, i: (b * nq + i, h)),
            pl.BlockSpec((t, g * 2 * LANES), lambda b, h, i: (b * nq + jnp.minimum(i + 1, nq - 1), h)),
            pl.BlockSpec((1, g, seq, 2 * LANES), lambda b, h, i: (b, h, 0, 0)),
            pl.BlockSpec((1, g, nq, VT_ROWS, t), lambda b, h, i: (b, h, 0, 0, 0)),
            pl.BlockSpec(mask.shape, lambda b, h, i: (0, 0)),
        ],
        out_specs=pl.BlockSpec((t, g * MLA_V_DIM), lambda b, h, i: (b * nq + i, h)),
        out_shape=jax.ShapeDtypeStruct((batch * seq, nh * MLA_V_DIM), BF16),
        scratch_shapes=[
            pltpu.VMEM((g, 1, t), F32), pltpu.VMEM((g, VT_ROWS, t), F32),
            pltpu.VMEM((2, g, t, t), F32), pltpu.VMEM((2, g, 1, t), F32),
        ],
        compiler_params=_cparams(("arbitrary", "arbitrary", "arbitrary")),
        name="b_attn",
    )(q, q, k, vt, mask)


def _post_kernel(h_ref, o_ref, wo_ref, fg_ref, win_ref, cw_ref, cb_ref, wout_ref, pg_ref, wg_ref,
                 p_ref, wp_ref, out_ref, carry_sc, *, tiles_per_batch, d_ff):
    tm = h_ref.shape[0]
    ts = tm // 2
    halves = (slice(0, ts), slice(ts, tm))

    @pl.when((pl.program_id(0) % tiles_per_batch) == 0)
    def _():
        carry_sc[...] = jnp.zeros(carry_sc.shape, F32)

    h1 = [h_ref[r, :] + jnp.dot(o_ref[r, :], wo_ref[...], preferred_element_type=F32) for r in halves]
    hn = [_rms_rows(x, fg_ref[...]).astype(BF16) for x in h1]

    def up(lo, width):
        return [[jnp.dot(x, win_ref[:, c:c + width], preferred_element_type=F32)
                 for c in (lo, d_ff + lo)] for x in hn]

    def conv(u_halves, lo, width):
        prev = carry_sc[:, lo:lo + width]
        cw = cw_ref[:, lo:lo + width]
        outs = []
        for u in u_halves:
            ext = jnp.concatenate([prev, u], axis=0)
            outs.append(cb_ref[:, lo:lo + width] + ext[6:6 + ts] * cw[0:1] + ext[7:7 + ts] * cw[1:2]
                        + u * cw[2:3])
            prev = u[ts - 8:, :]
        carry_sc[:, lo:lo + width] = prev
        return outs

    chunks = [(lo, min(FF_CHUNK, d_ff - lo)) for lo in range(0, d_ff, FF_CHUNK)]
    u_next = up(*chunks[0])
    acc = [None, None]
    for ci, (lo, width) in enumerate(chunks):
        u = u_next
        if ci + 1 < len(chunks):
            u_next = up(*chunks[ci + 1])
        a = conv([u[0][0], u[1][0]], lo, width)
        gt = conv([u[0][1], u[1][1]], d_ff + lo, width)
        for i in range(2):
            act = (gt[i] * jax.nn.sigmoid(gt[i]) * a[i]).astype(BF16)
            part = jnp.dot(act, wout_ref[lo:lo + width, :], preferred_element_type=F32)
            acc[i] = part if acc[i] is None else acc[i] + part
    pe = [jnp.dot(p_ref[r, :].astype(BF16), wp_ref[...], preferred_element_type=F32) for r in halves]
    for i, r in enumerate(halves):
        h2 = h1[i] + acc[i]
        gate = jax.nn.sigmoid(jnp.dot(_rms_rows(h2, pg_ref[...]).astype(BF16), wg_ref[...],
                                      preferred_element_type=F32))
        out_ref[r, :] = h2 + pe[i] * gate


def _post(h, o, wo, fg, win, cw, cb, wout, pg, wg, p_i, wp, *, seq):
    t, d = h.shape
    tm = POST_TM
    d_ff = wout.shape[0]
    return pl.pallas_call(
        functools.partial(_post_kernel, tiles_per_batch=seq // tm, d_ff=d_ff),
        grid=(t // tm,),
        in_specs=[
            pl.BlockSpec((tm, d), lambda i: (i, 0)),
            pl.BlockSpec((tm, o.shape[1]), lambda i: (i, 0)),
            _resident(wo.shape), _resident(fg.shape), _resident(win.shape), _resident(cw.shape),
            _resident(cb.shape), _resident(wout.shape), _resident(pg.shape), _resident(wg.shape),
            pl.BlockSpec((tm, p_i.shape[1]), lambda i: (i, 0)),
            _resident(wp.shape),
        ],
        out_specs=pl.BlockSpec((tm, d), lambda i: (i, 0)),
        out_shape=jax.ShapeDtypeStruct((t, d), F32),
        scratch_shapes=[pltpu.VMEM((8, 2 * d_ff), F32)],
        compiler_params=_cparams(("arbitrary",)),
        name="post",
    )(h, o, wo, fg, win, cw, cb, wout, pg, wg, p_i, wp)


def _t5_bucket_table():
    n = np.arange(T5_MAX_DISTANCE + 1)
    max_exact = T5_BUCKETS // 2
    log_ratio = np.log(np.maximum(n, 1) / max_exact) / math.log(T5_MAX_DISTANCE / max_exact)
    large = np.minimum(max_exact + (log_ratio * (T5_BUCKETS - max_exact)).astype(np.int32),
                       T5_BUCKETS - 1)
    return np.where(n < max_exact, n, large).astype(np.int32)


def _bias_tiles(rel_table, t):
    assert t >= T5_MAX_DISTANCE
    nh = rel_table.shape[1]
    rel = rel_table[_t5_bucket_table()]
    rel = (rel - rel[T5_MAX_DISTANCE]).astype(F32)
    near = rel[:T5_MAX_DISTANCE]
    zeros = jnp.zeros((t - T5_MAX_DISTANCE, nh), F32)

    def toeplitz(v_pos, v_neg):
        v = jnp.concatenate([v_pos, jnp.zeros((1, nh), F32), v_neg], axis=0)
        u = jnp.tile(v, (t, 1))[:t * (2 * t - 1)]
        return u.reshape(t, 2 * t - 1, nh)[:, :t]

    b0 = toeplitz(jnp.concatenate([near, zeros], axis=0), jnp.full((t - 1, nh), NEG_INF, F32))
    b1 = toeplitz(jnp.zeros((t, nh), F32), jnp.concatenate([near[1:], zeros], axis=0))
    tiles = jnp.stack([b0, b1], axis=0).transpose(3, 0, 1, 2)
    return jnp.concatenate([tiles, tiles], axis=-1)


def _causal_tile(t):
    kk = np.arange(t)[:, None]
    qq = np.arange(t)[None, :]
    return jnp.asarray(np.where(qq >= kk, 0.0, NEG_INF), F32)


def _block_diag_ones(group):
    idx = np.arange(MXU_DIM) // group
    return jnp.asarray(idx[:, None] == idx[None, :], BF16)


def _swap_halves(v):
    half = v.shape[-1] // 2
    return jnp.concatenate([v[..., half:], v[..., :half]], axis=-1)


def kernel(x, p, positions, rel_bias_table, attn_norm, a_w_qkv, a_q_norm, a_k_norm, a_lam_q1, a_lam_k1, a_lam_q2, a_lam_k2, a_sub_norm, a_w_o, kv_norm, w_dkv, ckv_norm, w_ukv, k_nope_norm, k_pe_norm, b_w_dq, b_cq_norm, b_w_uq, b_q_nope_norm, b_q_pe_norm, b_w_o, ffn_norm, ffn_w_in, ffn_conv_w, ffn_conv_b, ffn_w_out, ple_norm, ple_w_proj, ple_w_gate):
    batch, seq, d = x.shape
    depth = p.shape[0]
    n_a = a_w_qkv.shape[0]
    t = batch * seq
    row = lambda v: v.reshape(1, -1).astype(F32)

    h = x.reshape(t, d)
    bd64 = _block_diag_ones(DIFF_QK_DIM)
    bias = _bias_tiles(rel_bias_table, A_T) * LOG2E
    cos_t, sin_t = _rope_tables(positions.reshape(t, 1))
    shared = None

    for i in range(depth):
        if i < n_a:
            lam_init = 0.8 - 0.6 * math.exp(-0.3 * i)
            q_scale = DIFF_QK_DIM ** -0.5 * LOG2E
            gqk = jnp.concatenate([jnp.tile(a_q_norm[i], 2 * DIFF_HEADS) * q_scale,
                                   jnp.tile(a_k_norm[i], 2 * DIFF_HEADS)]).reshape(1, -1)
            q, k, vt = _a_proj(h, row(attn_norm[i]), a_w_qkv[i].astype(BF16), gqk, bd64,
                               batch=batch, seq=seq, tk=A_T)
            lam_params = jnp.stack([a_lam_q1[i], a_lam_k1[i], a_lam_q2[i], a_lam_k2[i]])
            o = _a_attn(q, k, vt, bias, lam_params, row(a_sub_norm[i]), lam_init=lam_init)
            o = o.reshape(t, -1)
            w_o = a_w_o[i]
        else:
            j = i - n_a
            if shared is None:
                lora = ckv_norm.shape[0]
                w_pe = w_dkv[:, lora:]
                w_pe_sw = _swap_halves(w_pe)
                wd = jnp.concatenate([w_dkv[:, :lora], w_pe, w_pe, w_pe_sw, w_pe_sw], axis=1)
                wu = w_ukv.reshape(lora, MLA_HEADS, MLA_QK_NOPE + MLA_V_DIM)
                wu = jnp.concatenate([wu[:, :, :MLA_QK_NOPE].reshape(lora, -1),
                                      wu[:, :, MLA_QK_NOPE:].reshape(lora, -1)], axis=1)
                shared = _kv_proj(h, row(kv_norm), wd.astype(BF16), row(ckv_norm),
                                  row(jnp.tile(k_pe_norm, 2)), row(jnp.tile(_swap_halves(k_pe_norm), 2)),
                                  cos_t, sin_t, wu.astype(BF16), row(k_nope_norm),
                                  batch=batch, seq=seq, tk=B_T)
                mask = _causal_tile(B_T)
            q_lora = b_w_uq.shape[1]
            wq = b_w_uq[j].reshape(q_lora, MLA_HEADS, MLA_QK_NOPE + MLA_QK_ROPE)
            wq_pe = wq[:, :, MLA_QK_NOPE:]
            wuq = jnp.concatenate([wq[:, :, :MLA_QK_NOPE].reshape(q_lora, -1),
                                   wq_pe.reshape(q_lora, -1),
                                   _swap_halves(wq_pe).reshape(q_lora, -1)], axis=1)
            scale = (MLA_QK_NOPE + MLA_QK_ROPE) ** -0.5 * LOG2E
            gp = b_q_pe_norm[j] * scale
            qf = _b_proj(h, row(attn_norm[i]), b_w_dq[j].astype(BF16), row(b_cq_norm[j]),
                         wuq.astype(BF16), row(b_q_nope_norm[j] * scale),
                         row(jnp.tile(gp, MLA_HEADS)), row(jnp.tile(_swap_halves(gp), MLA_HEADS)),
                         cos_t, sin_t, bd64)
            o = _b_attn(qf, shared[0], shared[1], mask, batch=batch, seq=seq)
            w_o = b_w_o[j]
        h = _post(h, o, w_o.astype(BF16), row(ffn_norm[i]), ffn_w_in[i].astype(BF16),
                  ffn_conv_w[i], row(ffn_conv_b[i]), ffn_w_out[i].astype(BF16), row(ple_norm[i]),
                  ple_w_gate[i].astype(BF16), p[i].reshape(t, -1), ple_w_proj[i].astype(BF16),
                  seq=seq)
    return h.reshape(batch, seq, d)
```

```python
import functools
import math

import numpy as np
import jax
import jax.numpy as jnp
from jax import lax
from jax.experimental import pallas as pl
from jax.experimental.pallas import tpu as pltpu

F32 = jnp.float32
BF16 = jnp.bfloat16

DIFF_HEADS = 8
DIFF_QK_DIM = 64
DIFF_V_DIM = 128
T5_BUCKETS = 32
T5_MAX_DISTANCE = 128
MLA_HEADS = 16
MLA_QK_NOPE = 128
MLA_QK_ROPE = 64
MLA_V_DIM = 128
ROPE_BASE = 10000.0
RMS_EPS = 1e-6
NEG_INF = -1e30
LOG2E = math.log2(math.e)

LANES = 128
MXU_DIM = 256
BF16_SUBLANES = 16
VT_ROWS = DIFF_V_DIM + BF16_SUBLANES
VMEM_LIMIT_BYTES = 56 * 1024 * 1024

PROJ_TM = 512
POST_TM = 512
A_T = 256
A_G = 4
B_T = 512
B_G = 4
FF_CHUNK = 768


def _cparams(sem):
    return pltpu.CompilerParams(dimension_semantics=sem, vmem_limit_bytes=VMEM_LIMIT_BYTES)


def _resident(shape):
    nd = len(shape)
    return pl.BlockSpec(shape, lambda *_: (0,) * nd, pipeline_mode=pl.Buffered(1))


def _rms_rows(x, g):
    ms = jnp.mean(x * x, axis=-1, keepdims=True)
    return x * lax.rsqrt(ms + RMS_EPS) * g


def _store_vt(vt_ref, v):
    vdim = v.shape[1]
    vt_ref[:vdim, :] = v.T.astype(BF16)
    vt_ref[vdim:, :] = jnp.ones((VT_ROWS - vdim, v.shape[0]), BF16)


def _group_mean_sq(x, bd, group):
    y = (x * x).astype(BF16)
    parts = [jnp.dot(y[:, c:c + MXU_DIM], bd, preferred_element_type=F32)
             for c in range(0, x.shape[1], MXU_DIM)]
    return jnp.concatenate(parts, axis=1) * (1.0 / group)


def _rope_kernel(pos_ref, cos_ref, sin_ref):
    pos = pos_ref[...].astype(F32)
    lane = lax.broadcasted_iota(jnp.int32, (1, LANES), 1)
    half = MLA_QK_ROPE // 2
    r = (lane % half).astype(F32)
    inv_freq = jnp.exp(-math.log(ROPE_BASE) * r * (2.0 / MLA_QK_ROPE))
    ang = pos * inv_freq
    cos_ref[...] = jnp.cos(ang)
    sin_ref[...] = jnp.where((lane % MLA_QK_ROPE) < half, -jnp.sin(ang), jnp.sin(ang))


def _rope_tables(pos_col):
    t = pos_col.shape[0]
    tm = 2048
    return pl.pallas_call(
        _rope_kernel,
        grid=(t // tm,),
        in_specs=[pl.BlockSpec((tm, 1), lambda i: (i, 0))],
        out_specs=[pl.BlockSpec((tm, LANES), lambda i: (i, 0))] * 2,
        out_shape=[jax.ShapeDtypeStruct((t, LANES), F32)] * 2,
        compiler_params=_cparams(("parallel",)),
        name="rope_tables",
    )(pos_col)


def _a_proj_kernel(h_ref, g_ref, w_ref, gqk_ref, bd_ref, q_ref, k_ref, vt_ref, *, tk):
    tm = h_ref.shape[0]
    qk_w = DIFF_HEADS * 2 * DIFF_QK_DIM
    hn = _rms_rows(h_ref[...], g_ref[...]).astype(BF16)
    qkv = jnp.dot(hn, w_ref[...], preferred_element_type=F32)
    qk = qkv[:, :2 * qk_w]
    ms = _group_mean_sq(qk, bd_ref[...], DIFF_QK_DIM)
    qkn = qk * lax.rsqrt(ms + RMS_EPS) * gqk_ref[...]
    for h in range(DIFF_HEADS):
        q_ref[0, h] = qkn[:, h * LANES:(h + 1) * LANES].astype(BF16)
        k_ref[0, h] = qkn[:, qk_w + h * LANES:qk_w + (h + 1) * LANES].astype(BF16)
        v_h = qkv[:, 2 * qk_w + h * LANES:2 * qk_w + (h + 1) * LANES]
        for c in range(tm // tk):
            _store_vt(vt_ref.at[0, h, c], v_h[c * tk:(c + 1) * tk])


def _a_proj(h, g, w_qkv, gqk, bd, *, batch, seq, tk):
    tm = PROJ_TM
    nt = seq // tm
    nh = DIFF_HEADS
    d = h.shape[1]
    return pl.pallas_call(
        functools.partial(_a_proj_kernel, tk=tk),
        grid=(batch * nt,),
        in_specs=[
            pl.BlockSpec((tm, d), lambda i: (i, 0)),
            _resident(g.shape), _resident(w_qkv.shape), _resident(gqk.shape), _resident(bd.shape),
        ],
        out_specs=[
            pl.BlockSpec((1, nh, tm, LANES), lambda i: (i // nt, 0, i % nt, 0)),
            pl.BlockSpec((1, nh, tm, LANES), lambda i: (i // nt, 0, i % nt, 0)),
            pl.BlockSpec((1, nh, tm // tk, VT_ROWS, tk), lambda i: (i // nt, 0, i % nt, 0, 0)),
        ],
        out_shape=[
            jax.ShapeDtypeStruct((batch, nh, seq, LANES), BF16),
            jax.ShapeDtypeStruct((batch, nh, seq, LANES), BF16),
            jax.ShapeDtypeStruct((batch, nh, seq // tk, VT_ROWS, tk), BF16),
        ],
        compiler_params=_cparams(("parallel",)),
        name="a_proj",
    )(h, g, w_qkv, gqk, bd)


def _scores(k, qz, s_ref, mb_ref):
    s = lax.dot_general(k, qz, (((1,), (1,)), ((), ())), preferred_element_type=F32)
    s_ref[...] = s
    mb_ref[...] = jnp.max(s, axis=0, keepdims=True)


def _softmax_pv(s_ref, mb_ref, extra, vt, m_sc, acc_sc):
    m_old = m_sc[...]
    if extra is None:
        s = s_ref[...]
        m_new = jnp.maximum(m_old, mb_ref[...])
    else:
        s = s_ref[...] + extra
        m_new = jnp.maximum(m_old, jnp.max(s, axis=0, keepdims=True))
    alpha = jnp.exp2(m_old - m_new)
    p = jnp.exp2(s - m_new)
    acc_sc[...] = alpha * acc_sc[...] + jnp.dot(vt, p.astype(BF16), preferred_element_type=F32)
    m_sc[...] = m_new


def _flash_init(m_sc, acc_sc):
    m_sc[...] = jnp.full(m_sc.shape, NEG_INF, F32)
    acc_sc[...] = jnp.zeros(acc_sc.shape, F32)


def _causal_pipeline(n_far, qk, sm, tails):
    def body(jj, carry):
        kj = 2 * jj
        qk(1, kj + 1)
        sm(0, kj, None)
        qk(0, kj + 2)
        sm(1, kj + 1, None)
        return carry

    lax.fori_loop(0, n_far // 2, body, 0)

    for cond, blocks in tails:
        @pl.when(cond)
        def _(blocks=blocks):
            for n, (kj, kind) in enumerate(blocks):
                if n + 1 < len(blocks):
                    qk(1 - n % 2, blocks[n + 1][0])
                sm(n % 2, kj, kind)


def _a_attn_kernel(q_ref, qn_ref, k_ref, vt_ref, bias_ref, lam_ref, sg_ref, o_ref, m_sc, acc_sc, s_sc,
                   mb_sc, *, lam_init, t, g):
    qi = pl.program_id(2)

    def stacked(ref):
        out = []
        for gi in range(g):
            q = ref[0, gi]
            lane = lax.broadcasted_iota(jnp.int32, q.shape, 1)
            zero = jnp.zeros_like(q)
            out.append(jnp.concatenate([jnp.where(lane < DIFF_QK_DIM, q, zero),
                                        jnp.where(lane >= DIFF_QK_DIM, q, zero)], axis=0))
        return out

    qz = stacked(q_ref)
    _flash_init(m_sc, acc_sc)

    def qk(slot, kj, qs=qz):
        for gi in range(g):
            k = k_ref[0, gi, pl.ds(pl.multiple_of(kj * t, t), t), :]
            _scores(k, qs[gi], s_sc.at[slot, gi], mb_sc.at[slot, gi])

    @pl.when(qi == 0)
    def _():
        qk(0, 0)

    def sm(slot, kj, kind):
        for gi in range(g):
            extra = None if kind is None else bias_ref[gi, kind]
            _softmax_pv(s_sc.at[slot, gi], mb_sc.at[slot, gi], extra, vt_ref[0, gi, kj],
                        m_sc.at[gi], acc_sc.at[gi])

    n_far = jnp.maximum(qi - 1, 0)
    odd = n_far % 2 == 1
    _causal_pipeline(n_far, qk, sm, [
        (qi == 0, [(qi, 0)]),
        ((qi >= 1) & jnp.logical_not(odd), [(qi - 1, 1), (qi, 0)]),
        (odd, [(qi - 2, None), (qi - 1, 1), (qi, 0)]),
    ])

    qk(0, 0, stacked(qn_ref))

    lp = lam_ref[...]
    lam = (jnp.exp(jnp.sum(lp[0:1] * lp[1:2], keepdims=True))
           - jnp.exp(jnp.sum(lp[2:3] * lp[3:4], keepdims=True)) + lam_init)
    for gi in range(g):
        acc = acc_sc[gi, :DIFF_V_DIM]
        inv = 1.0 / acc_sc[gi, DIFF_V_DIM:DIFF_V_DIM + 1]
        o_t = acc[:, :t] * inv[:, :t] - lam * (acc[:, t:] * inv[:, t:])
        o = _rms_rows(o_t.T, sg_ref[...]) * (1.0 - lam_init)
        o_ref[0, :, gi * DIFF_V_DIM:(gi + 1) * DIFF_V_DIM] = o.astype(BF16)


def _a_attn(q, k, vt, bias, lam_params, sub_gain, *, lam_init):
    batch, nh, seq, _ = q.shape
    t = A_T
    g = A_G
    nq = seq // t
    return pl.pallas_call(
        functools.partial(_a_attn_kernel, lam_init=lam_init, t=t, g=g),
        grid=(batch, nh // g, nq),
        in_specs=[
            pl.BlockSpec((1, g, t, LANES), lambda b, h, i: (b, h, i, 0)),
            pl.BlockSpec((1, g, t, LANES), lambda b, h, i: (b, h, jnp.minimum(i + 1, nq - 1), 0)),
            pl.BlockSpec((1, g, seq, LANES), lambda b, h, i: (b, h, 0, 0)),
            pl.BlockSpec((1, g, nq, VT_ROWS, t), lambda b, h, i: (b, h, 0, 0, 0)),
            pl.BlockSpec((g, 2, t, 2 * t), lambda b, h, i: (h, 0, 0, 0)),
            pl.BlockSpec(lam_params.shape, lambda b, h, i: (0, 0)),
            pl.BlockSpec(sub_gain.shape, lambda b, h, i: (0, 0)),
        ],
        out_specs=pl.BlockSpec((1, t, g * DIFF_V_DIM), lambda b, h, i: (b, i, h)),
        out_shape=jax.ShapeDtypeStruct((batch, seq, nh * DIFF_V_DIM), BF16),
        scratch_shapes=[
            pltpu.VMEM((g, 1, 2 * t), F32), pltpu.VMEM((g, VT_ROWS, 2 * t), F32),
            pltpu.VMEM((2, g, t, 2 * t), F32), pltpu.VMEM((2, g, 1, 2 * t), F32),
        ],
        compiler_params=_cparams(("arbitrary", "arbitrary", "arbitrary")),
        name="a_attn",
    )(q, q, k, vt, bias, lam_params, sub_gain)


def _kv_proj_kernel(h_ref, g_ref, wd_ref, cg_ref, pg_ref, pgs_ref, cos_ref, sin_ref, wu_ref, kg_ref,
                    k_ref, vt_ref, *, tk):
    tm = h_ref.shape[0]
    lora = cg_ref.shape[1]
    hn = _rms_rows(h_ref[...], g_ref[...]).astype(BF16)
    ckv = jnp.dot(hn, wd_ref[...], preferred_element_type=F32)
    c_kv = _rms_rows(ckv[:, :lora], cg_ref[...]).astype(BF16)
    pe = ckv[:, lora:lora + LANES]
    pe_sw = ckv[:, lora + LANES:]
    r = lax.rsqrt(jnp.mean(pe * pe, axis=-1, keepdims=True) + RMS_EPS)
    k_pe = ((pe * r * pg_ref[...]) * cos_ref[...] + (pe_sw * r * pgs_ref[...]) * sin_ref[...])
    lane = lax.broadcasted_iota(jnp.int32, k_pe.shape, 1)
    zero = jnp.zeros_like(k_pe)
    pe_even = jnp.where(lane < MLA_QK_ROPE, k_pe, zero).astype(BF16)
    pe_odd = jnp.where(lane >= MLA_QK_ROPE, k_pe, zero).astype(BF16)
    kv = jnp.dot(c_kv, wu_ref[...], preferred_element_type=F32)
    nope_w = MLA_HEADS * MLA_QK_NOPE
    for h in range(MLA_HEADS):
        kn = _rms_rows(kv[:, h * LANES:(h + 1) * LANES], kg_ref[...])
        k_ref[0, h, :, :LANES] = kn.astype(BF16)
        k_ref[0, h, :, LANES:] = pe_even if h % 2 == 0 else pe_odd
        v_h = kv[:, nope_w + h * LANES:nope_w + (h + 1) * LANES]
        for c in range(tm // tk):
            _store_vt(vt_ref.at[0, h, c], v_h[c * tk:(c + 1) * tk])


def _kv_proj(h, g, wd, cg, pg, pgs, cos_t, sin_t, wu, kg, *, batch, seq, tk):
    tm = PROJ_TM
    nt = seq // tm
    nh = MLA_HEADS
    d = h.shape[1]
    return pl.pallas_call(
        functools.partial(_kv_proj_kernel, tk=tk),
        grid=(batch * nt,),
        in_specs=[
            pl.BlockSpec((tm, d), lambda i: (i, 0)),
            _resident(g.shape), _resident(wd.shape), _resident(cg.shape),
            _resident(pg.shape), _resident(pgs.shape),
            pl.BlockSpec((tm, LANES), lambda i: (i, 0)),
            pl.BlockSpec((tm, LANES), lambda i: (i, 0)),
            _resident(wu.shape), _resident(kg.shape),
        ],
        out_specs=[
            pl.BlockSpec((1, nh, tm, 2 * LANES), lambda i: (i // nt, 0, i % nt, 0)),
            pl.BlockSpec((1, nh, tm // tk, VT_ROWS, tk), lambda i: (i // nt, 0, i % nt, 0, 0)),
        ],
        out_shape=[
            jax.ShapeDtypeStruct((batch, nh, seq, 2 * LANES), BF16),
            jax.ShapeDtypeStruct((batch, nh, seq // tk, VT_ROWS, tk), BF16),
        ],
        compiler_params=_cparams(("parallel",)),
        name="kv_proj",
    )(h, g, wd, cg, pg, pgs, cos_t, sin_t, wu, kg)


def _b_proj_kernel(h_ref, g_ref, wdq_ref, cqg_ref, wuq_ref, gn_ref, gp_ref, gps_ref, cos_ref, sin_ref,
                   bd_ref, q_ref):
    nope_w = MLA_HEADS * MLA_QK_NOPE
    pe_w = MLA_HEADS * MLA_QK_ROPE
    hn = _rms_rows(h_ref[...], g_ref[...]).astype(BF16)
    cq = jnp.dot(hn, wdq_ref[...], preferred_element_type=F32)
    cqn = _rms_rows(cq, cqg_ref[...]).astype(BF16)
    q = jnp.dot(cqn, wuq_ref[...], preferred_element_type=F32)
    qp = q[:, nope_w:nope_w + pe_w]
    qps = q[:, nope_w + pe_w:]
    r = lax.rsqrt(_group_mean_sq(qp, bd_ref[...], MLA_QK_ROPE) + RMS_EPS)
    reps = pe_w // LANES
    cos_t = jnp.concatenate([cos_ref[...]] * reps, axis=1)
    sin_t = jnp.concatenate([sin_ref[...]] * reps, axis=1)
    rot = (qp * r * gp_ref[...]) * cos_t + (qps * r * gps_ref[...]) * sin_t
    lane = lax.broadcasted_iota(jnp.int32, (q.shape[0], LANES), 1)
    zero = jnp.zeros((q.shape[0], LANES), F32)
    for h in range(MLA_HEADS):
        qn = _rms_rows(q[:, h * LANES:(h + 1) * LANES], gn_ref[...])
        pair = rot[:, (h // 2) * LANES:(h // 2 + 1) * LANES]
        keep = (lane < MLA_QK_ROPE) if h % 2 == 0 else (lane >= MLA_QK_ROPE)
        q_ref[:, h * 2 * LANES:h * 2 * LANES + LANES] = qn.astype(BF16)
        q_ref[:, h * 2 * LANES + LANES:(h + 1) * 2 * LANES] = jnp.where(keep, pair, zero).astype(BF16)


def _b_proj(h, g, wdq, cqg, wuq, gn, gp, gps, cos_t, sin_t, bd):
    t, d = h.shape
    tm = PROJ_TM
    qw = MLA_HEADS * 2 * LANES
    return pl.pallas_call(
        _b_proj_kernel,
        grid=(t // tm,),
        in_specs=[
            pl.BlockSpec((tm, d), lambda i: (i, 0)),
            _resident(g.shape), _resident(wdq.shape), _resident(cqg.shape), _resident(wuq.shape),
            _resident(gn.shape), _resident(gp.shape), _resident(gps.shape),
            pl.BlockSpec((tm, LANES), lambda i: (i, 0)),
            pl.BlockSpec((tm, LANES), lambda i: (i, 0)),
            _resident(bd.shape),
        ],
        out_specs=pl.BlockSpec((tm, qw), lambda i: (i, 0)),
        out_shape=jax.ShapeDtypeStruct((t, qw), BF16),
        compiler_params=_cparams(("parallel",)),
        name="b_proj",
    )(h, g, wdq, cqg, wuq, gn, gp, gps, cos_t, sin_t, bd)


def _b_attn_kernel(q_ref, qn_ref, k_ref, vt_ref, mask_ref, o_ref, m_sc, acc_sc, s_sc, mb_sc, *, t, g):
    qi = pl.program_id(2)
    qw = 2 * LANES
    _flash_init(m_sc, acc_sc)

    def qk(slot, kj, ref=q_ref):
        for gi in range(g):
            k = k_ref[0, gi, pl.ds(pl.multiple_of(kj * t, t), t), :]
            _scores(k, ref[:, gi * qw:(gi + 1) * qw], s_sc.at[slot, gi], mb_sc.at[slot, gi])

    @pl.when(qi == 0)
    def _():
        qk(0, 0)

    def sm(slot, kj, kind):
        for gi in range(g):
            extra = None if kind is None else mask_ref[...]
            _softmax_pv(s_sc.at[slot, gi], mb_sc.at[slot, gi], extra, vt_ref[0, gi, kj],
                        m_sc.at[gi], acc_sc.at[gi])

    odd = qi % 2 == 1
    _causal_pipeline(qi, qk, sm, [
        (jnp.logical_not(odd), [(qi, 0)]),
        (odd, [(qi - 1, None), (qi, 0)]),
    ])

    qk(0, 0, qn_ref)

    for gi in range(g):
        o_t = acc_sc[gi, :MLA_V_DIM] * (1.0 / acc_sc[gi, MLA_V_DIM:MLA_V_DIM + 1])
        o_ref[:, gi * MLA_V_DIM:(gi + 1) * MLA_V_DIM] = o_t.T.astype(BF16)


def _b_attn(q, k, vt, mask, *, batch, seq):
    nh = MLA_HEADS
    t = B_T
    g = B_G
    nq = seq // t
    return pl.pallas_call(
        functools.partial(_b_attn_kernel, t=t, g=g),
        grid=(batch, nh // g, nq),
        in_specs=[
            pl.BlockSpec((t, g * 2 * LANES), lambda b, h, i: (b * nq + i, h)),
            pl.BlockSpec((t, g * 2 * LANES), lambda b, h, i: (b * nq + jnp.minimum(i + 1, nq - 1), h)),
            pl.BlockSpec((1, g, seq, 2 * LANES), lambda b, h, i: (b, h, 0, 0)),
            pl.BlockSpec((1, g, nq, VT_ROWS, t), lambda b, h, i: (b, h, 0, 0, 0)),
            pl.BlockSpec(mask.shape, lambda b, h, i: (0, 0)),
        ],
        out_specs=pl.BlockSpec((t, g * MLA_V_DIM), lambda b, h, i: (b * nq + i, h)),
        out_shape=jax.ShapeDtypeStruct((batch * seq, nh * MLA_V_DIM), BF16),
        scratch_shapes=[
            pltpu.VMEM((g, 1, t), F32), pltpu.VMEM((g, VT_ROWS, t), F32),
            pltpu.VMEM((2, g, t, t), F32), pltpu.VMEM((2, g, 1, t), F32),
        ],
        compiler_params=_cparams(("arbitrary", "arbitrary", "arbitrary")),
        name="b_attn",
    )(q, q, k, vt, mask)


def _post_kernel(h_ref, o_ref, wo_ref, fg_ref, win_ref, cw_ref, cb_ref, wout_ref, pg_ref, wg_ref,
                 p_ref, wp_ref, out_ref, carry_sc, *, tiles_per_batch, d_ff):
    tm = h_ref.shape[0]
    ts = tm // 2
    halves = (slice(0, ts), slice(ts, tm))

    @pl.when((pl.program_id(0) % tiles_per_batch) == 0)
    def _():
        carry_sc[...] = jnp.zeros(carry_sc.shape, F32)

    h1 = [h_ref[r, :] + jnp.dot(o_ref[r, :], wo_ref[...], preferred_element_type=F32) for r in halves]
    hn = [_rms_rows(x, fg_ref[...]).astype(BF16) for x in h1]

    def up(lo, width):
        return [[jnp.dot(x, win_ref[:, c:c + width], preferred_element_type=F32)
                 for c in (lo, d_ff + lo)] for x in hn]

    def conv(u_halves, lo, width):
        prev = carry_sc[:, lo:lo + width]
        cw = cw_ref[:, lo:lo + width]
        outs = []
        for u in u_halves:
            ext = jnp.concatenate([prev, u], axis=0)
            outs.append(cb_ref[:, lo:lo + width] + ext[6:6 + ts] * cw[0:1] + ext[7:7 + ts] * cw[1:2]
                        + u * cw[2:3])
            prev = u[ts - 8:, :]
        carry_sc[:, lo:lo + width] = prev
        return outs

    chunks = [(lo, min(FF_CHUNK, d_ff - lo)) for lo in range(0, d_ff, FF_CHUNK)]
    u_next = up(*chunks[0])
    acc = [None, None]
    for ci, (lo, width) in enumerate(chunks):
        u = u_next
        if ci + 1 < len(chunks):
            u_next = up(*chunks[ci + 1])
        a = conv([u[0][0], u[1][0]], lo, width)
        gt = conv([u[0][1], u[1][1]], d_ff + lo, width)
        for i in range(2):
            act = (gt[i] * jax.nn.sigmoid(gt[i]) * a[i]).astype(BF16)
            part = jnp.dot(act, wout_ref[lo:lo + width, :], preferred_element_type=F32)
            acc[i] = part if acc[i] is None else acc[i] + part
    pe = [jnp.dot(p_ref[r, :].astype(BF16), wp_ref[...], preferred_element_type=F32) for r in halves]
    for i, r in enumerate(halves):
        h2 = h1[i] + acc[i]
        gate = jax.nn.sigmoid(jnp.dot(_rms_rows(h2, pg_ref[...]).astype(BF16), wg_ref[...],
                                      preferred_element_type=F32))
        out_ref[r, :] = h2 + pe[i] * gate


def _post(h, o, wo, fg, win, cw, cb, wout, pg, wg, p_all, wp, *, seq, layer):
    t, d = h.shape
    tm = POST_TM
    d_ff = wout.shape[0]
    return pl.pallas_call(
        functools.partial(_post_kernel, tiles_per_batch=seq // tm, d_ff=d_ff),
        grid=(t // tm,),
        in_specs=[
            pl.BlockSpec((tm, d), lambda i: (i, 0)),
            pl.BlockSpec((tm, o.shape[1]), lambda i: (i, 0)),
            _resident(wo.shape), _resident(fg.shape), _resident(win.shape), _resident(cw.shape),
            _resident(cb.shape), _resident(wout.shape), _resident(pg.shape), _resident(wg.shape),
            pl.BlockSpec((None, tm, p_all.shape[2]), lambda i: (layer, i, 0)),
            _resident(wp.shape),
        ],
        out_specs=pl.BlockSpec((tm, d), lambda i: (i, 0)),
        out_shape=jax.ShapeDtypeStruct((t, d), F32),
        scratch_shapes=[pltpu.VMEM((8, 2 * d_ff), F32)],
        compiler_params=_cparams(("arbitrary",)),
        name="post",
    )(h, o, wo, fg, win, cw, cb, wout, pg, wg, p_all, wp)


def _t5_bucket_table():
    n = np.arange(T5_MAX_DISTANCE + 1)
    max_exact = T5_BUCKETS // 2
    log_ratio = np.log(np.maximum(n, 1) / max_exact) / math.log(T5_MAX_DISTANCE / max_exact)
    large = np.minimum(max_exact + (log_ratio * (T5_BUCKETS - max_exact)).astype(np.int32),
                       T5_BUCKETS - 1)
    return np.where(n < max_exact, n, large).astype(np.int32)


def _bias_tiles(rel_table, t):
    assert t >= T5_MAX_DISTANCE
    nh = rel_table.shape[1]
    rel = rel_table[_t5_bucket_table()]
    rel = (rel - rel[T5_MAX_DISTANCE]).astype(F32).T
    near = rel[:, :T5_MAX_DISTANCE]
    zeros = jnp.zeros((nh, t - T5_MAX_DISTANCE), F32)

    def toeplitz(v_pos, v_neg):
        v = jnp.concatenate([v_pos, jnp.zeros((nh, 1), F32), v_neg], axis=1)
        u = jnp.tile(v, (1, t))[:, :t * (2 * t - 1)]
        return u.reshape(nh, t, 2 * t - 1)[:, :, :t]

    b0 = toeplitz(jnp.concatenate([near, zeros], axis=1), jnp.full((nh, t - 1), NEG_INF, F32))
    b1 = toeplitz(jnp.zeros((nh, t), F32), jnp.concatenate([near[:, 1:], zeros], axis=1))
    tiles = jnp.stack([b0, b1], axis=1)
    return jnp.concatenate([tiles, tiles], axis=-1)


def _causal_tile(t):
    kk = np.arange(t)[:, None]
    qq = np.arange(t)[None, :]
    return jnp.asarray(np.where(qq >= kk, 0.0, NEG_INF), F32)


def _block_diag_ones(group):
    idx = np.arange(MXU_DIM) // group
    return jnp.asarray(idx[:, None] == idx[None, :], BF16)


def _swap_halves(v):
    half = v.shape[-1] // 2
    return jnp.concatenate([v[..., half:], v[..., :half]], axis=-1)


def kernel(x, p, positions, rel_bias_table, attn_norm, a_w_qkv, a_q_norm, a_k_norm, a_lam_q1, a_lam_k1, a_lam_q2, a_lam_k2, a_sub_norm, a_w_o, kv_norm, w_dkv, ckv_norm, w_ukv, k_nope_norm, k_pe_norm, b_w_dq, b_cq_norm, b_w_uq, b_q_nope_norm, b_q_pe_norm, b_w_o, ffn_norm, ffn_w_in, ffn_conv_w, ffn_conv_b, ffn_w_out, ple_norm, ple_w_proj, ple_w_gate):
    batch, seq, d = x.shape
    depth = p.shape[0]
    n_a = a_w_qkv.shape[0]
    t = batch * seq
    row = lambda v: v.reshape(1, -1).astype(F32)

    h = x.reshape(t, d)
    p_rows = p.reshape(depth, t, -1)
    bd64 = _block_diag_ones(DIFF_QK_DIM)
    bias = _bias_tiles(rel_bias_table, A_T) * LOG2E
    cos_t, sin_t = _rope_tables(positions.reshape(t, 1))
    shared = None

    for i in range(depth):
        if i < n_a:
            lam_init = 0.8 - 0.6 * math.exp(-0.3 * i)
            q_scale = DIFF_QK_DIM ** -0.5 * LOG2E
            gqk = jnp.concatenate([jnp.tile(a_q_norm[i], 2 * DIFF_HEADS) * q_scale,
                                   jnp.tile(a_k_norm[i], 2 * DIFF_HEADS)]).reshape(1, -1)
            q, k, vt = _a_proj(h, row(attn_norm[i]), a_w_qkv[i].astype(BF16), gqk, bd64,
                               batch=batch, seq=seq, tk=A_T)
            lam_params = jnp.stack([a_lam_q1[i], a_lam_k1[i], a_lam_q2[i], a_lam_k2[i]])
            o = _a_attn(q, k, vt, bias, lam_params, row(a_sub_norm[i]), lam_init=lam_init)
            o = o.reshape(t, -1)
            w_o = a_w_o[i]
        else:
            j = i - n_a
            if shared is None:
                lora = ckv_norm.shape[0]
                w_pe = w_dkv[:, lora:]
                w_pe_sw = _swap_halves(w_pe)
                wd = jnp.concatenate([w_dkv[:, :lora], w_pe, w_pe, w_pe_sw, w_pe_sw], axis=1)
                wu = w_ukv.reshape(lora, MLA_HEADS, MLA_QK_NOPE + MLA_V_DIM)
                wu = jnp.concatenate([wu[:, :, :MLA_QK_NOPE].reshape(lora, -1),
                                      wu[:, :, MLA_QK_NOPE:].reshape(lora, -1)], axis=1)
                shared = _kv_proj(h, row(kv_norm), wd.astype(BF16), row(ckv_norm),
                                  row(jnp.tile(k_pe_norm, 2)), row(jnp.tile(_swap_halves(k_pe_norm), 2)),
                                  cos_t, sin_t, wu.astype(BF16), row(k_nope_norm),
                                  batch=batch, seq=seq, tk=B_T)
                mask = _causal_tile(B_T)
            q_lora = b_w_uq.shape[1]
            wq = b_w_uq[j].reshape(q_lora, MLA_HEADS, MLA_QK_NOPE + MLA_QK_ROPE)
            wq_pe = wq[:, :, MLA_QK_NOPE:]
            wuq = jnp.concatenate([wq[:, :, :MLA_QK_NOPE].reshape(q_lora, -1),
                                   wq_pe.reshape(q_lora, -1),
                                   _swap_halves(wq_pe).reshape(q_lora, -1)], axis=1)
            scale = (MLA_QK_NOPE + MLA_QK_ROPE) ** -0.5 * LOG2E
            gp = b_q_pe_norm[j] * scale
            qf = _b_proj(h, row(attn_norm[i]), b_w_dq[j].astype(BF16), row(b_cq_norm[j]),
                         wuq.astype(BF16), row(b_q_nope_norm[j] * scale),
                         row(jnp.tile(gp, MLA_HEADS)), row(jnp.tile(_swap_halves(gp), MLA_HEADS)),
                         cos_t, sin_t, bd64)
            o = _b_attn(qf, shared[0], shared[1], mask, batch=batch, seq=seq)
            w_o = b_w_o[j]
        h = _post(h, o, w_o.astype(BF16), row(ffn_norm[i]), ffn_w_in[i].astype(BF16),
                  ffn_conv_w[i], row(ffn_conv_b[i]), ffn_w_out[i].astype(BF16), row(ple_norm[i]),
                  ple_w_gate[i].astype(BF16), p_rows, ple_w_proj[i].astype(BF16), seq=seq, layer=i)
    return h.reshape(batch, seq, d)
```

```python
import functools
import math

import numpy as np
import jax
import jax.numpy as jnp
from jax import lax
from jax.experimental import pallas as pl
from jax.experimental.pallas import tpu as pltpu

F32 = jnp.float32
BF16 = jnp.bfloat16

DIFF_HEADS = 8
DIFF_QK_DIM = 64
DIFF_V_DIM = 128
T5_BUCKETS = 32
T5_MAX_DISTANCE = 128
MLA_HEADS = 16
MLA_QK_NOPE = 128
MLA_QK_ROPE = 64
MLA_V_DIM = 128
ROPE_BASE = 10000.0
RMS_EPS = 1e-6
NEG_INF = -1e30
LOG2E = math.log2(math.e)

LANES = 128
MXU_DIM = 256
F32_SUBLANES = 8
BF16_SUBLANES = 16
VT_ROWS = DIFF_V_DIM + BF16_SUBLANES
VMEM_LIMIT_BYTES = 56 * 1024 * 1024

ROPE_TM = 2048
PROJ_TM = 1024
POST_TM = 512
A_T = 256
A_G = 4
B_T = 512
B_G = 4
FF_CHUNK = 768


def _cparams(sem):
    return pltpu.CompilerParams(dimension_semantics=sem, vmem_limit_bytes=VMEM_LIMIT_BYTES)


def _resident(shape):
    nd = len(shape)
    return pl.BlockSpec(shape, lambda *_: (0,) * nd, pipeline_mode=pl.Buffered(1))


def _rms_rows(x, g):
    ms = jnp.mean(x * x, axis=-1, keepdims=True)
    return x * lax.rsqrt(ms + RMS_EPS) * g


def _store_vt(vt_ref, v):
    vdim = v.shape[1]
    vt_ref[:vdim, :] = v.T.astype(BF16)
    vt_ref[vdim:, :] = jnp.ones((VT_ROWS - vdim, v.shape[0]), BF16)


def _group_mean_sq(x, bd, group):
    y = (x * x).astype(BF16)
    parts = [jnp.dot(y[:, c:c + MXU_DIM], bd, preferred_element_type=F32)
             for c in range(0, x.shape[1], MXU_DIM)]
    return jnp.concatenate(parts, axis=1) * (1.0 / group)


def _rope_kernel(pos_ref, cos_ref, sin_ref):
    pos = pos_ref[...].astype(F32)
    lane = lax.broadcasted_iota(jnp.int32, (1, LANES), 1)
    half = MLA_QK_ROPE // 2
    r = (lane % half).astype(F32)
    inv_freq = jnp.exp(-math.log(ROPE_BASE) * r * (2.0 / MLA_QK_ROPE))
    ang = pos * inv_freq
    cos_ref[...] = jnp.cos(ang)
    sin_ref[...] = jnp.where((lane % MLA_QK_ROPE) < half, -jnp.sin(ang), jnp.sin(ang))


def _rope_tables(pos_col):
    t = pos_col.shape[0]
    tm = ROPE_TM
    return pl.pallas_call(
        _rope_kernel,
        grid=(t // tm,),
        in_specs=[pl.BlockSpec((tm, 1), lambda i: (i, 0))],
        out_specs=[pl.BlockSpec((tm, LANES), lambda i: (i, 0))] * 2,
        out_shape=[jax.ShapeDtypeStruct((t, LANES), F32)] * 2,
        compiler_params=_cparams(("parallel",)),
        name="rope_tables",
    )(pos_col)


def _a_proj_kernel(h_ref, g_ref, w_ref, gqk_ref, bd_ref, q_ref, k_ref, vt_ref, *, tk):
    tm = h_ref.shape[0]
    qk_w = DIFF_HEADS * 2 * DIFF_QK_DIM
    hn = _rms_rows(h_ref[...], g_ref[...]).astype(BF16)
    qkv = jnp.dot(hn, w_ref[...], preferred_element_type=F32)
    qk = qkv[:, :2 * qk_w]
    ms = _group_mean_sq(qk, bd_ref[...], DIFF_QK_DIM)
    qkn = qk * lax.rsqrt(ms + RMS_EPS) * gqk_ref[...]
    for h in range(DIFF_HEADS):
        q_ref[0, h] = qkn[:, h * LANES:(h + 1) * LANES].astype(BF16)
        k_ref[0, h] = qkn[:, qk_w + h * LANES:qk_w + (h + 1) * LANES].astype(BF16)
        v_h = qkv[:, 2 * qk_w + h * LANES:2 * qk_w + (h + 1) * LANES]
        for c in range(tm // tk):
            _store_vt(vt_ref.at[0, h, c], v_h[c * tk:(c + 1) * tk])


def _a_proj(h, g, w_qkv, gqk, bd, *, batch, seq, tk):
    tm = PROJ_TM
    nt = seq // tm
    nh = DIFF_HEADS
    d = h.shape[1]
    return pl.pallas_call(
        functools.partial(_a_proj_kernel, tk=tk),
        grid=(batch * nt,),
        in_specs=[
            pl.BlockSpec((tm, d), lambda i: (i, 0)),
            _resident(g.shape), _resident(w_qkv.shape), _resident(gqk.shape), _resident(bd.shape),
        ],
        out_specs=[
            pl.BlockSpec((1, nh, tm, LANES), lambda i: (i // nt, 0, i % nt, 0)),
            pl.BlockSpec((1, nh, tm, LANES), lambda i: (i // nt, 0, i % nt, 0)),
            pl.BlockSpec((1, nh, tm // tk, VT_ROWS, tk), lambda i: (i // nt, 0, i % nt, 0, 0)),
        ],
        out_shape=[
            jax.ShapeDtypeStruct((batch, nh, seq, LANES), BF16),
            jax.ShapeDtypeStruct((batch, nh, seq, LANES), BF16),
            jax.ShapeDtypeStruct((batch, nh, seq // tk, VT_ROWS, tk), BF16),
        ],
        compiler_params=_cparams(("parallel",)),
        name="a_proj",
    )(h, g, w_qkv, gqk, bd)


def _scores(k, qz, s_ref, mb_ref):
    s = lax.dot_general(k, qz, (((1,), (1,)), ((), ())), preferred_element_type=F32)
    s_ref[...] = s
    mb_ref[...] = jnp.max(s, axis=0, keepdims=True)


def _softmax_pv(s_ref, mb_ref, extra, vt, m_sc, acc_sc):
    m_old = m_sc[...]
    if extra is None:
        s = s_ref[...]
        m_new = jnp.maximum(m_old, mb_ref[...])
    else:
        s = s_ref[...] + extra
        m_new = jnp.maximum(m_old, jnp.max(s, axis=0, keepdims=True))
    alpha = jnp.exp2(m_old - m_new)
    p = jnp.exp2(s - m_new)
    acc_sc[...] = alpha * acc_sc[...] + jnp.dot(vt, p.astype(BF16), preferred_element_type=F32)
    m_sc[...] = m_new


def _flash_init(m_sc, acc_sc):
    m_sc[...] = jnp.full(m_sc.shape, NEG_INF, F32)
    acc_sc[...] = jnp.zeros(acc_sc.shape, F32)


def _causal_pipeline(n_far, qk, sm, tails):
    def body(jj, carry):
        kj = 2 * jj
        qk(1, kj + 1)
        sm(0, kj, None)
        qk(0, kj + 2)
        sm(1, kj + 1, None)
        return carry

    lax.fori_loop(0, n_far // 2, body, 0)

    for cond, blocks in tails:
        @pl.when(cond)
        def _(blocks=blocks):
            for n, (kj, kind) in enumerate(blocks):
                if n + 1 < len(blocks):
                    qk(1 - n % 2, blocks[n + 1][0])
                sm(n % 2, kj, kind)


def _a_attn_kernel(q_ref, qn_ref, k_ref, vt_ref, bias_ref, lam_ref, sg_ref, o_ref, m_sc, acc_sc, s_sc,
                   mb_sc, *, lam_init, t, g):
    qi = pl.program_id(2)

    def stacked(ref):
        out = []
        for gi in range(g):
            q = ref[0, gi]
            lane = lax.broadcasted_iota(jnp.int32, q.shape, 1)
            zero = jnp.zeros_like(q)
            out.append(jnp.concatenate([jnp.where(lane < DIFF_QK_DIM, q, zero),
                                        jnp.where(lane >= DIFF_QK_DIM, q, zero)], axis=0))
        return out

    qz = stacked(q_ref)
    _flash_init(m_sc, acc_sc)

    def qk(slot, kj, qs=qz):
        for gi in range(g):
            k = k_ref[0, gi, pl.ds(pl.multiple_of(kj * t, t), t), :]
            _scores(k, qs[gi], s_sc.at[slot, gi], mb_sc.at[slot, gi])

    @pl.when(qi == 0)
    def _():
        qk(0, 0)

    def sm(slot, kj, kind):
        for gi in range(g):
            extra = None if kind is None else bias_ref[gi, kind]
            _softmax_pv(s_sc.at[slot, gi], mb_sc.at[slot, gi], extra, vt_ref[0, gi, kj],
                        m_sc.at[gi], acc_sc.at[gi])

    n_far = jnp.maximum(qi - 1, 0)
    odd = n_far % 2 == 1
    _causal_pipeline(n_far, qk, sm, [
        (qi == 0, [(qi, 0)]),
        ((qi >= 1) & jnp.logical_not(odd), [(qi - 1, 1), (qi, 0)]),
        (odd, [(qi - 2, None), (qi - 1, 1), (qi, 0)]),
    ])

    qk(0, 0, stacked(qn_ref))

    lp = lam_ref[...]
    lam = (jnp.exp(jnp.sum(lp[0:1] * lp[1:2], keepdims=True))
           - jnp.exp(jnp.sum(lp[2:3] * lp[3:4], keepdims=True)) + lam_init)
    for gi in range(g):
        acc = acc_sc[gi, :DIFF_V_DIM]
        inv = 1.0 / acc_sc[gi, DIFF_V_DIM:DIFF_V_DIM + 1]
        o_t = acc[:, :t] * inv[:, :t] - lam * (acc[:, t:] * inv[:, t:])
        o = _rms_rows(o_t.T, sg_ref[...]) * (1.0 - lam_init)
        o_ref[0, :, gi * DIFF_V_DIM:(gi + 1) * DIFF_V_DIM] = o.astype(BF16)


def _a_attn(q, k, vt, bias, lam_params, sub_gain, *, lam_init):
    batch, nh, seq, _ = q.shape
    t = A_T
    g = A_G
    nq = seq // t
    return pl.pallas_call(
        functools.partial(_a_attn_kernel, lam_init=lam_init, t=t, g=g),
        grid=(batch, nh // g, nq),
        in_specs=[
            pl.BlockSpec((1, g, t, LANES), lambda b, h, i: (b, h, i, 0)),
            pl.BlockSpec((1, g, t, LANES), lambda b, h, i: (b, h, jnp.minimum(i + 1, nq - 1), 0)),
            pl.BlockSpec((1, g, seq, LANES), lambda b, h, i: (b, h, 0, 0)),
            pl.BlockSpec((1, g, nq, VT_ROWS, t), lambda b, h, i: (b, h, 0, 0, 0)),
            pl.BlockSpec((g, 2, t, 2 * t), lambda b, h, i: (h, 0, 0, 0)),
            pl.BlockSpec(lam_params.shape, lambda b, h, i: (0, 0)),
            pl.BlockSpec(sub_gain.shape, lambda b, h, i: (0, 0)),
        ],
        out_specs=pl.BlockSpec((1, t, g * DIFF_V_DIM), lambda b, h, i: (b, i, h)),
        out_shape=jax.ShapeDtypeStruct((batch, seq, nh * DIFF_V_DIM), BF16),
        scratch_shapes=[
            pltpu.VMEM((g, 1, 2 * t), F32), pltpu.VMEM((g, VT_ROWS, 2 * t), F32),
            pltpu.VMEM((2, g, t, 2 * t), F32), pltpu.VMEM((2, g, 1, 2 * t), F32),
        ],
        compiler_params=_cparams(("arbitrary", "arbitrary", "arbitrary")),
        name="a_attn",
    )(q, q, k, vt, bias, lam_params, sub_gain)


def _kv_proj_kernel(h_ref, g_ref, wd_ref, cg_ref, pg_ref, pgs_ref, cos_ref, sin_ref, wu_ref, kg_ref,
                    k_ref, vt_ref, *, tk):
    tm = h_ref.shape[0]
    lora = cg_ref.shape[1]
    hn = _rms_rows(h_ref[...], g_ref[...]).astype(BF16)
    ckv = jnp.dot(hn, wd_ref[...], preferred_element_type=F32)
    c_kv = _rms_rows(ckv[:, :lora], cg_ref[...]).astype(BF16)
    pe = ckv[:, lora:lora + LANES]
    pe_sw = ckv[:, lora + LANES:]
    r = lax.rsqrt(jnp.mean(pe * pe, axis=-1, keepdims=True) + RMS_EPS)
    k_pe = ((pe * r * pg_ref[...]) * cos_ref[...] + (pe_sw * r * pgs_ref[...]) * sin_ref[...])
    lane = lax.broadcasted_iota(jnp.int32, k_pe.shape, 1)
    zero = jnp.zeros_like(k_pe)
    pe_even = jnp.where(lane < MLA_QK_ROPE, k_pe, zero).astype(BF16)
    pe_odd = jnp.where(lane >= MLA_QK_ROPE, k_pe, zero).astype(BF16)
    kv = jnp.dot(c_kv, wu_ref[...], preferred_element_type=F32)
    nope_w = MLA_HEADS * MLA_QK_NOPE
    for h in range(MLA_HEADS):
        kn = _rms_rows(kv[:, h * LANES:(h + 1) * LANES], kg_ref[...])
        k_ref[0, h, :, :LANES] = kn.astype(BF16)
        k_ref[0, h, :, LANES:] = pe_even if h % 2 == 0 else pe_odd
        v_h = kv[:, nope_w + h * LANES:nope_w + (h + 1) * LANES]
        for c in range(tm // tk):
            _store_vt(vt_ref.at[0, h, c], v_h[c * tk:(c + 1) * tk])


def _kv_proj(h, g, wd, cg, pg, pgs, cos_t, sin_t, wu, kg, *, batch, seq, tk):
    tm = PROJ_TM
    nt = seq // tm
    nh = MLA_HEADS
    d = h.shape[1]
    return pl.pallas_call(
        functools.partial(_kv_proj_kernel, tk=tk),
        grid=(batch * nt,),
        in_specs=[
            pl.BlockSpec((tm, d), lambda i: (i, 0)),
            _resident(g.shape), _resident(wd.shape), _resident(cg.shape),
            _resident(pg.shape), _resident(pgs.shape),
            pl.BlockSpec((tm, LANES), lambda i: (i, 0)),
            pl.BlockSpec((tm, LANES), lambda i: (i, 0)),
            _resident(wu.shape), _resident(kg.shape),
        ],
        out_specs=[
            pl.BlockSpec((1, nh, tm, 2 * LANES), lambda i: (i // nt, 0, i % nt, 0)),
            pl.BlockSpec((1, nh, tm // tk, VT_ROWS, tk), lambda i: (i // nt, 0, i % nt, 0, 0)),
        ],
        out_shape=[
            jax.ShapeDtypeStruct((batch, nh, seq, 2 * LANES), BF16),
            jax.ShapeDtypeStruct((batch, nh, seq // tk, VT_ROWS, tk), BF16),
        ],
        compiler_params=_cparams(("parallel",)),
        name="kv_proj",
    )(h, g, wd, cg, pg, pgs, cos_t, sin_t, wu, kg)


def _b_proj_kernel(h_ref, g_ref, wdq_ref, cqg_ref, wuq_ref, gn_ref, gp_ref, gps_ref, cos_ref, sin_ref,
                   bd_ref, q_ref):
    nope_w = MLA_HEADS * MLA_QK_NOPE
    pe_w = MLA_HEADS * MLA_QK_ROPE
    hn = _rms_rows(h_ref[...], g_ref[...]).astype(BF16)
    cq = jnp.dot(hn, wdq_ref[...], preferred_element_type=F32)
    cqn = _rms_rows(cq, cqg_ref[...]).astype(BF16)
    q = jnp.dot(cqn, wuq_ref[...], preferred_element_type=F32)
    qp = q[:, nope_w:nope_w + pe_w]
    qps = q[:, nope_w + pe_w:]
    r = lax.rsqrt(_group_mean_sq(qp, bd_ref[...], MLA_QK_ROPE) + RMS_EPS)
    reps = pe_w // LANES
    cos_t = jnp.concatenate([cos_ref[...]] * reps, axis=1)
    sin_t = jnp.concatenate([sin_ref[...]] * reps, axis=1)
    rot = (qp * r * gp_ref[...]) * cos_t + (qps * r * gps_ref[...]) * sin_t
    lane = lax.broadcasted_iota(jnp.int32, (q.shape[0], LANES), 1)
    zero = jnp.zeros((q.shape[0], LANES), F32)
    for h in range(MLA_HEADS):
        qn = _rms_rows(q[:, h * LANES:(h + 1) * LANES], gn_ref[...])
        pair = rot[:, (h // 2) * LANES:(h // 2 + 1) * LANES]
        keep = (lane < MLA_QK_ROPE) if h % 2 == 0 else (lane >= MLA_QK_ROPE)
        q_ref[:, h * 2 * LANES:h * 2 * LANES + LANES] = qn.astype(BF16)
        q_ref[:, h * 2 * LANES + LANES:(h + 1) * 2 * LANES] = jnp.where(keep, pair, zero).astype(BF16)


def _b_proj(h, g, wdq, cqg, wuq, gn, gp, gps, cos_t, sin_t, bd):
    t, d = h.shape
    tm = PROJ_TM
    qw = MLA_HEADS * 2 * LANES
    return pl.pallas_call(
        _b_proj_kernel,
        grid=(t // tm,),
        in_specs=[
            pl.BlockSpec((tm, d), lambda i: (i, 0)),
            _resident(g.shape), _resident(wdq.shape), _resident(cqg.shape), _resident(wuq.shape),
            _resident(gn.shape), _resident(gp.shape), _resident(gps.shape),
            pl.BlockSpec((tm, LANES), lambda i: (i, 0)),
            pl.BlockSpec((tm, LANES), lambda i: (i, 0)),
            _resident(bd.shape),
        ],
        out_specs=pl.BlockSpec((tm, qw), lambda i: (i, 0)),
        out_shape=jax.ShapeDtypeStruct((t, qw), BF16),
        compiler_params=_cparams(("parallel",)),
        name="b_proj",
    )(h, g, wdq, cqg, wuq, gn, gp, gps, cos_t, sin_t, bd)


def _b_attn_kernel(q_ref, qn_ref, k_ref, vt_ref, mask_ref, o_ref, m_sc, acc_sc, s_sc, mb_sc, *, t, g):
    qi = pl.program_id(2)
    qw = 2 * LANES
    _flash_init(m_sc, acc_sc)

    def qk(slot, kj, ref=q_ref):
        for gi in range(g):
            k = k_ref[0, gi, pl.ds(pl.multiple_of(kj * t, t), t), :]
            _scores(k, ref[:, gi * qw:(gi + 1) * qw], s_sc.at[slot, gi], mb_sc.at[slot, gi])

    @pl.when(qi == 0)
    def _():
        qk(0, 0)

    def sm(slot, kj, kind):
        for gi in range(g):
            extra = None if kind is None else mask_ref[...]
            _softmax_pv(s_sc.at[slot, gi], mb_sc.at[slot, gi], extra, vt_ref[0, gi, kj],
                        m_sc.at[gi], acc_sc.at[gi])

    odd = qi % 2 == 1
    _causal_pipeline(qi, qk, sm, [
        (jnp.logical_not(odd), [(qi, 0)]),
        (odd, [(qi - 1, None), (qi, 0)]),
    ])

    qk(0, 0, qn_ref)

    for gi in range(g):
        o_t = acc_sc[gi, :MLA_V_DIM] * (1.0 / acc_sc[gi, MLA_V_DIM:MLA_V_DIM + 1])
        o_ref[:, gi * MLA_V_DIM:(gi + 1) * MLA_V_DIM] = o_t.T.astype(BF16)


def _b_attn(q, k, vt, mask, *, batch, seq):
    nh = MLA_HEADS
    t = B_T
    g = B_G
    nq = seq // t
    return pl.pallas_call(
        functools.partial(_b_attn_kernel, t=t, g=g),
        grid=(batch, nh // g, nq),
        in_specs=[
            pl.BlockSpec((t, g * 2 * LANES), lambda b, h, i: (b * nq + i, h)),
            pl.BlockSpec((t, g * 2 * LANES), lambda b, h, i: (b * nq + jnp.minimum(i + 1, nq - 1), h)),
            pl.BlockSpec((1, g, seq, 2 * LANES), lambda b, h, i: (b, h, 0, 0)),
            pl.BlockSpec((1, g, nq, VT_ROWS, t), lambda b, h, i: (b, h, 0, 0, 0)),
            pl.BlockSpec(mask.shape, lambda b, h, i: (0, 0)),
        ],
        out_specs=pl.BlockSpec((t, g * MLA_V_DIM), lambda b, h, i: (b * nq + i, h)),
        out_shape=jax.ShapeDtypeStruct((batch * seq, nh * MLA_V_DIM), BF16),
        scratch_shapes=[
            pltpu.VMEM((g, 1, t), F32), pltpu.VMEM((g, VT_ROWS, t), F32),
            pltpu.VMEM((2, g, t, t), F32), pltpu.VMEM((2, g, 1, t), F32),
        ],
        compiler_params=_cparams(("arbitrary", "arbitrary", "arbitrary")),
        name="b_attn",
    )(q, q, k, vt, mask)


def _post_kernel(h_ref, o_ref, wo_ref, fg_ref, win_ref, cw_ref, cb_ref, wout_ref, pg_ref, wg_ref,
                 p_ref, wp_ref, out_ref, carry_sc, *, tiles_per_batch, d_ff):
    tm = h_ref.shape[0]
    ts = tm // 2
    halves = (slice(0, ts), slice(ts, tm))
    hist = F32_SUBLANES

    @pl.when((pl.program_id(0) % tiles_per_batch) == 0)
    def _():
        carry_sc[...] = jnp.zeros(carry_sc.shape, F32)

    h1 = [h_ref[r, :] + jnp.dot(o_ref[r, :], wo_ref[...], preferred_element_type=F32) for r in halves]
    hn = [_rms_rows(x, fg_ref[...]).astype(BF16) for x in h1]

    def up(lo, width):
        return [[jnp.dot(x, win_ref[:, c:c + width], preferred_element_type=F32)
                 for c in (lo, d_ff + lo)] for x in hn]

    def conv(u_halves, lo, width):
        prev = carry_sc[:, lo:lo + width]
        cw = cw_ref[:, lo:lo + width]
        outs = []
        for u in u_halves:
            ext = jnp.concatenate([prev, u], axis=0)
            outs.append(cb_ref[:, lo:lo + width] + ext[hist - 2:hist - 2 + ts] * cw[0:1]
                        + ext[hist - 1:hist - 1 + ts] * cw[1:2] + u * cw[2:3])
            prev = u[ts - hist:, :]
        carry_sc[:, lo:lo + width] = prev
        return outs

    chunks = [(lo, min(FF_CHUNK, d_ff - lo)) for lo in range(0, d_ff, FF_CHUNK)]
    u_next = up(*chunks[0])
    acc = [None, None]
    for ci, (lo, width) in enumerate(chunks):
        u = u_next
        if ci + 1 < len(chunks):
            u_next = up(*chunks[ci + 1])
        a = conv([u[0][0], u[1][0]], lo, width)
        gt = conv([u[0][1], u[1][1]], d_ff + lo, width)
        for i in range(2):
            act = (gt[i] * jax.nn.sigmoid(gt[i]) * a[i]).astype(BF16)
            part = jnp.dot(act, wout_ref[lo:lo + width, :], preferred_element_type=F32)
            acc[i] = part if acc[i] is None else acc[i] + part
    pe = [jnp.dot(p_ref[r, :].astype(BF16), wp_ref[...], preferred_element_type=F32) for r in halves]
    for i, r in enumerate(halves):
        h2 = h1[i] + acc[i]
        gate = jax.nn.sigmoid(jnp.dot(_rms_rows(h2, pg_ref[...]).astype(BF16), wg_ref[...],
                                      preferred_element_type=F32))
        out_ref[r, :] = h2 + pe[i] * gate


def _post(h, o, wo, fg, win, cw, cb, wout, pg, wg, p_all, wp, *, seq, layer):
    t, d = h.shape
    tm = POST_TM
    d_ff = wout.shape[0]
    return pl.pallas_call(
        functools.partial(_post_kernel, tiles_per_batch=seq // tm, d_ff=d_ff),
        grid=(t // tm,),
        in_specs=[
            pl.BlockSpec((tm, d), lambda i: (i, 0)),
            pl.BlockSpec((tm, o.shape[1]), lambda i: (i, 0)),
            _resident(wo.shape), _resident(fg.shape), _resident(win.shape), _resident(cw.shape),
            _resident(cb.shape), _resident(wout.shape), _resident(pg.shape), _resident(wg.shape),
            pl.BlockSpec((None, tm, p_all.shape[2]), lambda i: (layer, i, 0)),
            _resident(wp.shape),
        ],
        out_specs=pl.BlockSpec((tm, d), lambda i: (i, 0)),
        out_shape=jax.ShapeDtypeStruct((t, d), F32),
        scratch_shapes=[pltpu.VMEM((F32_SUBLANES, 2 * d_ff), F32)],
        compiler_params=_cparams(("arbitrary",)),
        name="post",
    )(h, o, wo, fg, win, cw, cb, wout, pg, wg, p_all, wp)


def _t5_bucket_table():
    n = np.arange(T5_MAX_DISTANCE + 1)
    max_exact = T5_BUCKETS // 2
    log_ratio = np.log(np.maximum(n, 1) / max_exact) / math.log(T5_MAX_DISTANCE / max_exact)
    large = np.minimum(max_exact + (log_ratio * (T5_BUCKETS - max_exact)).astype(np.int32),
                       T5_BUCKETS - 1)
    return np.where(n < max_exact, n, large).astype(np.int32)


def _bias_tiles(rel_table, t):
    assert t >= T5_MAX_DISTANCE
    nh = rel_table.shape[1]
    rel = rel_table[_t5_bucket_table()]
    rel = (rel - rel[T5_MAX_DISTANCE]).astype(F32).T
    near = rel[:, :T5_MAX_DISTANCE]
    zeros = jnp.zeros((nh, t - T5_MAX_DISTANCE), F32)

    def toeplitz(v_pos, v_neg):
        v = jnp.concatenate([v_pos, jnp.zeros((nh, 1), F32), v_neg], axis=1)
        u = jnp.tile(v, (1, t))[:, :t * (2 * t - 1)]
        return u.reshape(nh, t, 2 * t - 1)[:, :, :t]

    b0 = toeplitz(jnp.concatenate([near, zeros], axis=1), jnp.full((nh, t - 1), NEG_INF, F32))
    b1 = toeplitz(jnp.zeros((nh, t), F32), jnp.concatenate([near[:, 1:], zeros], axis=1))
    tiles = jnp.stack([b0, b1], axis=1)
    return jnp.concatenate([tiles, tiles], axis=-1)


def _causal_tile(t):
    kk = np.arange(t)[:, None]
    qq = np.arange(t)[None, :]
    return jnp.asarray(np.where(qq >= kk, 0.0, NEG_INF), F32)


def _block_diag_ones(group):
    idx = np.arange(MXU_DIM) // group
    return jnp.asarray(idx[:, None] == idx[None, :], BF16)


def _swap_halves(v):
    half = v.shape[-1] // 2
    return jnp.concatenate([v[..., half:], v[..., :half]], axis=-1)


def kernel(x, p, positions, rel_bias_table, attn_norm, a_w_qkv, a_q_norm, a_k_norm, a_lam_q1, a_lam_k1, a_lam_q2, a_lam_k2, a_sub_norm, a_w_o, kv_norm, w_dkv, ckv_norm, w_ukv, k_nope_norm, k_pe_norm, b_w_dq, b_cq_norm, b_w_uq, b_q_nope_norm, b_q_pe_norm, b_w_o, ffn_norm, ffn_w_in, ffn_conv_w, ffn_conv_b, ffn_w_out, ple_norm, ple_w_proj, ple_w_gate):
    batch, seq, d = x.shape
    depth = p.shape[0]
    n_a = a_w_qkv.shape[0]
    t = batch * seq
    row = lambda v: v.reshape(1, -1).astype(F32)

    h = x.reshape(t, d)
    p_rows = p.reshape(depth, t, -1)
    bd64 = _block_diag_ones(DIFF_QK_DIM)
    bias = _bias_tiles(rel_bias_table, A_T) * LOG2E
    cos_t, sin_t = _rope_tables(positions.reshape(t, 1))
    shared = None

    for i in range(depth):
        if i < n_a:
            lam_init = 0.8 - 0.6 * math.exp(-0.3 * i)
            q_scale = DIFF_QK_DIM ** -0.5 * LOG2E
            gqk = jnp.concatenate([jnp.tile(a_q_norm[i], 2 * DIFF_HEADS) * q_scale,
                                   jnp.tile(a_k_norm[i], 2 * DIFF_HEADS)]).reshape(1, -1)
            q, k, vt = _a_proj(h, row(attn_norm[i]), a_w_qkv[i].astype(BF16), gqk, bd64,
                               batch=batch, seq=seq, tk=A_T)
            lam_params = jnp.stack([a_lam_q1[i], a_lam_k1[i], a_lam_q2[i], a_lam_k2[i]])
            o = _a_attn(q, k, vt, bias, lam_params, row(a_sub_norm[i]), lam_init=lam_init)
            o = o.reshape(t, -1)
            w_o = a_w_o[i]
        else:
            j = i - n_a
            if shared is None:
                lora = ckv_norm.shape[0]
                w_pe = w_dkv[:, lora:]
                w_pe_sw = _swap_halves(w_pe)
                wd = jnp.concatenate([w_dkv[:, :lora], w_pe, w_pe, w_pe_sw, w_pe_sw], axis=1)
                wu = w_ukv.reshape(lora, MLA_HEADS, MLA_QK_NOPE + MLA_V_DIM)
                wu = jnp.concatenate([wu[:, :, :MLA_QK_NOPE].reshape(lora, -1),
                                      wu[:, :, MLA_QK_NOPE:].reshape(lora, -1)], axis=1)
                shared = _kv_proj(h, row(kv_norm), wd.astype(BF16), row(ckv_norm),
                                  row(jnp.tile(k_pe_norm, 2)), row(jnp.tile(_swap_halves(k_pe_norm), 2)),
                                  cos_t, sin_t, wu.astype(BF16), row(k_nope_norm),
                                  batch=batch, seq=seq, tk=B_T)
                mask = _causal_tile(B_T)
            q_lora = b_w_uq.shape[1]
            wq = b_w_uq[j].reshape(q_lora, MLA_HEADS, MLA_QK_NOPE + MLA_QK_ROPE)
            wq_pe = wq[:, :, MLA_QK_NOPE:]
            wuq = jnp.concatenate([wq[:, :, :MLA_QK_NOPE].reshape(q_lora, -1),
                                   wq_pe.reshape(q_lora, -1),
                                   _swap_halves(wq_pe).reshape(q_lora, -1)], axis=1)
            scale = (MLA_QK_NOPE + MLA_QK_ROPE) ** -0.5 * LOG2E
            gp = b_q_pe_norm[j] * scale
            qf = _b_proj(h, row(attn_norm[i]), b_w_dq[j].astype(BF16), row(b_cq_norm[j]),
                         wuq.astype(BF16), row(b_q_nope_norm[j] * scale),
                         row(jnp.tile(gp, MLA_HEADS)), row(jnp.tile(_swap_halves(gp), MLA_HEADS)),
                         cos_t, sin_t, bd64)
            o = _b_attn(qf, shared[0], shared[1], mask, batch=batch, seq=seq)
            w_o = b_w_o[j]
        h = _post(h, o, w_o.astype(BF16), row(ffn_norm[i]), ffn_w_in[i].astype(BF16),
                  ffn_conv_w[i], row(ffn_conv_b[i]), ffn_w_out[i].astype(BF16), row(ple_norm[i]),
                  ple_w_gate[i].astype(BF16), p_rows, ple_w_proj[i].astype(BF16), seq=seq, layer=i)
    return h.reshape(batch, seq, d)
```

```python
import functools
import math

import numpy as np
import jax
import jax.numpy as jnp
from jax import lax
from jax.experimental import pallas as pl
from jax.experimental.pallas import tpu as pltpu

F32 = jnp.float32
BF16 = jnp.bfloat16

DIFF_HEADS = 8
DIFF_QK_DIM = 64
DIFF_V_DIM = 128
T5_BUCKETS = 32
T5_MAX_DISTANCE = 128
MLA_HEADS = 16
MLA_QK_NOPE = 128
MLA_QK_ROPE = 64
MLA_V_DIM = 128
ROPE_BASE = 10000.0
RMS_EPS = 1e-6
NEG_INF = -1e30
LOG2E = math.log2(math.e)

LANES = 128
MXU_DIM = 256
F32_SUBLANES = 8
BF16_SUBLANES = 16
VT_ROWS = DIFF_V_DIM + BF16_SUBLANES
VMEM_LIMIT_BYTES = 56 * 1024 * 1024

ROPE_TM = 2048
PROJ_TM = 1024
POST_TM = 512
A_T = 256
A_G = 4
B_T = 512
B_G = 4
FF_CHUNK = 768


def _cparams(sem):
    return pltpu.CompilerParams(dimension_semantics=sem, vmem_limit_bytes=VMEM_LIMIT_BYTES)


def _resident(shape):
    nd = len(shape)
    return pl.BlockSpec(shape, lambda *_: (0,) * nd, pipeline_mode=pl.Buffered(1))


def _rms_rows(x, g):
    ms = jnp.mean(x * x, axis=-1, keepdims=True)
    return x * lax.rsqrt(ms + RMS_EPS) * g


def _store_vt(vt_ref, v):
    vdim = v.shape[1]
    vt_ref[:vdim, :] = v.T.astype(BF16)
    vt_ref[vdim:, :] = jnp.ones((VT_ROWS - vdim, v.shape[0]), BF16)


def _group_mean_sq(x, bd, group):
    y = (x * x).astype(BF16)
    parts = [jnp.dot(y[:, c:c + MXU_DIM], bd, preferred_element_type=F32)
             for c in range(0, x.shape[1], MXU_DIM)]
    return jnp.concatenate(parts, axis=1) * (1.0 / group)


def _rope_kernel(pos_ref, cos_ref, sin_ref):
    pos = pos_ref[...].astype(F32)
    lane = lax.broadcasted_iota(jnp.int32, (1, LANES), 1)
    half = MLA_QK_ROPE // 2
    r = (lane % half).astype(F32)
    inv_freq = jnp.exp(-math.log(ROPE_BASE) * r * (2.0 / MLA_QK_ROPE))
    ang = pos * inv_freq
    cos_ref[...] = jnp.cos(ang)
    sin_ref[...] = jnp.where((lane % MLA_QK_ROPE) < half, -jnp.sin(ang), jnp.sin(ang))


def _rope_tables(pos_col):
    t = pos_col.shape[0]
    tm = ROPE_TM
    return pl.pallas_call(
        _rope_kernel,
        grid=(t // tm,),
        in_specs=[pl.BlockSpec((tm, 1), lambda i: (i, 0))],
        out_specs=[pl.BlockSpec((tm, LANES), lambda i: (i, 0))] * 2,
        out_shape=[jax.ShapeDtypeStruct((t, LANES), F32)] * 2,
        compiler_params=_cparams(("parallel",)),
        name="rope_tables",
    )(pos_col)


def _a_proj_kernel(h_ref, g_ref, w_ref, gqk_ref, bd_ref, q_ref, k_ref, vt_ref, *, tk):
    tm = h_ref.shape[0]
    qk_w = DIFF_HEADS * 2 * DIFF_QK_DIM
    hn = _rms_rows(h_ref[...], g_ref[...]).astype(BF16)
    qkv = jnp.dot(hn, w_ref[...], preferred_element_type=F32)
    qk = qkv[:, :2 * qk_w]
    ms = _group_mean_sq(qk, bd_ref[...], DIFF_QK_DIM)
    qkn = qk * lax.rsqrt(ms + RMS_EPS) * gqk_ref[...]
    for h in range(DIFF_HEADS):
        q_ref[0, h] = qkn[:, h * LANES:(h + 1) * LANES].astype(BF16)
        k_ref[0, h] = qkn[:, qk_w + h * LANES:qk_w + (h + 1) * LANES].astype(BF16)
        v_h = qkv[:, 2 * qk_w + h * LANES:2 * qk_w + (h + 1) * LANES]
        for c in range(tm // tk):
            _store_vt(vt_ref.at[0, h, c], v_h[c * tk:(c + 1) * tk])


def _a_proj(h, g, w_qkv, gqk, bd, *, batch, seq, tk):
    tm = PROJ_TM
    nt = seq // tm
    nh = DIFF_HEADS
    d = h.shape[1]
    return pl.pallas_call(
        functools.partial(_a_proj_kernel, tk=tk),
        grid=(batch * nt,),
        in_specs=[
            pl.BlockSpec((tm, d), lambda i: (i, 0)),
            _resident(g.shape), _resident(w_qkv.shape), _resident(gqk.shape), _resident(bd.shape),
        ],
        out_specs=[
            pl.BlockSpec((1, nh, tm, LANES), lambda i: (i // nt, 0, i % nt, 0)),
            pl.BlockSpec((1, nh, tm, LANES), lambda i: (i // nt, 0, i % nt, 0)),
            pl.BlockSpec((1, nh, tm // tk, VT_ROWS, tk), lambda i: (i // nt, 0, i % nt, 0, 0)),
        ],
        out_shape=[
            jax.ShapeDtypeStruct((batch, nh, seq, LANES), BF16),
            jax.ShapeDtypeStruct((batch, nh, seq, LANES), BF16),
            jax.ShapeDtypeStruct((batch, nh, seq // tk, VT_ROWS, tk), BF16),
        ],
        compiler_params=_cparams(("parallel",)),
        name="a_proj",
    )(h, g, w_qkv, gqk, bd)


def _scores(k, qz, s_ref, mb_ref):
    s = lax.dot_general(k, qz, (((1,), (1,)), ((), ())), preferred_element_type=F32)
    s_ref[...] = s
    mb_ref[...] = jnp.max(s, axis=0, keepdims=True)


def _softmax_pv(s_ref, mb_ref, extra, vt, m_sc, acc_sc):
    m_old = m_sc[...]
    if extra is None:
        s = s_ref[...]
        m_new = jnp.maximum(m_old, mb_ref[...])
    else:
        s = s_ref[...] + extra
        m_new = jnp.maximum(m_old, jnp.max(s, axis=0, keepdims=True))
    alpha = jnp.exp2(m_old - m_new)
    p = jnp.exp2(s - m_new)
    acc_sc[...] = alpha * acc_sc[...] + jnp.dot(vt, p.astype(BF16), preferred_element_type=F32)
    m_sc[...] = m_new


def _flash_init(m_sc, acc_sc):
    m_sc[...] = jnp.full(m_sc.shape, NEG_INF, F32)
    acc_sc[...] = jnp.zeros(acc_sc.shape, F32)


def _causal_pipeline(n_far, qk, sm, tails):
    def body(jj, carry):
        kj = 2 * jj
        qk(1, kj + 1, None)
        sm(0, kj, None)
        qk(0, kj + 2, None)
        sm(1, kj + 1, None)
        return carry

    lax.fori_loop(0, n_far // 2, body, 0)

    for cond, blocks in tails:
        @pl.when(cond)
        def _(blocks=blocks):
            for n, (kj, kind) in enumerate(blocks):
                if n + 1 < len(blocks):
                    qk(1 - n % 2, *blocks[n + 1])
                sm(n % 2, kj, kind)


def _a_attn_kernel(q_ref, qn_ref, k_ref, vt_ref, bias_ref, lam_ref, sg_ref, o_ref, m_sc, acc_sc, s_sc,
                   mb_sc, *, lam_init, t, g):
    qi = pl.program_id(2)

    def stacked(ref):
        out = []
        for gi in range(g):
            q = ref[0, gi]
            lane = lax.broadcasted_iota(jnp.int32, q.shape, 1)
            zero = jnp.zeros_like(q)
            out.append(jnp.concatenate([jnp.where(lane < DIFF_QK_DIM, q, zero),
                                        jnp.where(lane >= DIFF_QK_DIM, q, zero)], axis=0))
        return out

    qz = stacked(q_ref)
    _flash_init(m_sc, acc_sc)

    def qk(slot, kj, kind, qs=qz):
        del kind
        for gi in range(g):
            k = k_ref[0, gi, pl.ds(pl.multiple_of(kj * t, t), t), :]
            _scores(k, qs[gi], s_sc.at[slot, gi], mb_sc.at[slot, gi])

    @pl.when(qi == 0)
    def _():
        qk(0, 0, None)

    def sm(slot, kj, kind):
        for gi in range(g):
            extra = None if kind is None else bias_ref[gi, kind]
            _softmax_pv(s_sc.at[slot, gi], mb_sc.at[slot, gi], extra, vt_ref[0, gi, kj],
                        m_sc.at[gi], acc_sc.at[gi])

    n_far = jnp.maximum(qi - 1, 0)
    odd = n_far % 2 == 1
    _causal_pipeline(n_far, qk, sm, [
        (qi == 0, [(qi, 0)]),
        ((qi >= 1) & jnp.logical_not(odd), [(qi - 1, 1), (qi, 0)]),
        (odd, [(qi - 2, None), (qi - 1, 1), (qi, 0)]),
    ])

    qk(0, 0, None, stacked(qn_ref))

    lp = lam_ref[...]
    lam = (jnp.exp(jnp.sum(lp[0:1] * lp[1:2], keepdims=True))
           - jnp.exp(jnp.sum(lp[2:3] * lp[3:4], keepdims=True)) + lam_init)
    for gi in range(g):
        acc = acc_sc[gi, :DIFF_V_DIM]
        inv = 1.0 / acc_sc[gi, DIFF_V_DIM:DIFF_V_DIM + 1]
        o_t = acc[:, :t] * inv[:, :t] - lam * (acc[:, t:] * inv[:, t:])
        o = _rms_rows(o_t.T, sg_ref[...]) * (1.0 - lam_init)
        o_ref[0, :, gi * DIFF_V_DIM:(gi + 1) * DIFF_V_DIM] = o.astype(BF16)


def _a_attn(q, k, vt, bias, lam_params, sub_gain, *, lam_init):
    batch, nh, seq, _ = q.shape
    t = A_T
    g = A_G
    nq = seq // t
    return pl.pallas_call(
        functools.partial(_a_attn_kernel, lam_init=lam_init, t=t, g=g),
        grid=(batch, nh // g, nq),
        in_specs=[
            pl.BlockSpec((1, g, t, LANES), lambda b, h, i: (b, h, i, 0)),
            pl.BlockSpec((1, g, t, LANES), lambda b, h, i: (b, h, jnp.minimum(i + 1, nq - 1), 0)),
            pl.BlockSpec((1, g, seq, LANES), lambda b, h, i: (b, h, 0, 0)),
            pl.BlockSpec((1, g, nq, VT_ROWS, t), lambda b, h, i: (b, h, 0, 0, 0)),
            pl.BlockSpec((g, 2, t, 2 * t), lambda b, h, i: (h, 0, 0, 0)),
            pl.BlockSpec(lam_params.shape, lambda b, h, i: (0, 0)),
            pl.BlockSpec(sub_gain.shape, lambda b, h, i: (0, 0)),
        ],
        out_specs=pl.BlockSpec((1, t, g * DIFF_V_DIM), lambda b, h, i: (b, i, h)),
        out_shape=jax.ShapeDtypeStruct((batch, seq, nh * DIFF_V_DIM), BF16),
        scratch_shapes=[
            pltpu.VMEM((g, 1, 2 * t), F32), pltpu.VMEM((g, VT_ROWS, 2 * t), F32),
            pltpu.VMEM((2, g, t, 2 * t), F32), pltpu.VMEM((2, g, 1, 2 * t), F32),
        ],
        compiler_params=_cparams(("arbitrary", "arbitrary", "arbitrary")),
        name="a_attn",
    )(q, q, k, vt, bias, lam_params, sub_gain)


def _kv_proj_kernel(h_ref, g_ref, wd_ref, cg_ref, pg_ref, pgs_ref, cos_ref, sin_ref, wu_ref, kg_ref,
                    k_ref, vt_ref, *, tk):
    tm = h_ref.shape[0]
    lora = cg_ref.shape[1]
    hn = _rms_rows(h_ref[...], g_ref[...]).astype(BF16)
    ckv = jnp.dot(hn, wd_ref[...], preferred_element_type=F32)
    c_kv = _rms_rows(ckv[:, :lora], cg_ref[...]).astype(BF16)
    pe = ckv[:, lora:lora + LANES]
    pe_sw = ckv[:, lora + LANES:]
    r = lax.rsqrt(jnp.mean(pe * pe, axis=-1, keepdims=True) + RMS_EPS)
    k_pe = ((pe * r * pg_ref[...]) * cos_ref[...] + (pe_sw * r * pgs_ref[...]) * sin_ref[...])
    lane = lax.broadcasted_iota(jnp.int32, k_pe.shape, 1)
    zero = jnp.zeros_like(k_pe)
    pe_even = jnp.where(lane < MLA_QK_ROPE, k_pe, zero).astype(BF16)
    pe_odd = jnp.where(lane >= MLA_QK_ROPE, k_pe, zero).astype(BF16)
    kv = jnp.dot(c_kv, wu_ref[...], preferred_element_type=F32)
    nope_w = MLA_HEADS * MLA_QK_NOPE
    for h in range(MLA_HEADS):
        kn = _rms_rows(kv[:, h * LANES:(h + 1) * LANES], kg_ref[...])
        k_ref[0, h, :, :LANES] = kn.astype(BF16)
        k_ref[0, h, :, LANES:] = pe_even if h % 2 == 0 else pe_odd
        v_h = kv[:, nope_w + h * LANES:nope_w + (h + 1) * LANES]
        for c in range(tm // tk):
            _store_vt(vt_ref.at[0, h, c], v_h[c * tk:(c + 1) * tk])


def _kv_proj(h, g, wd, cg, pg, pgs, cos_t, sin_t, wu, kg, *, batch, seq, tk):
    tm = PROJ_TM
    nt = seq // tm
    nh = MLA_HEADS
    d = h.shape[1]
    return pl.pallas_call(
        functools.partial(_kv_proj_kernel, tk=tk),
        grid=(batch * nt,),
        in_specs=[
            pl.BlockSpec((tm, d), lambda i: (i, 0)),
            _resident(g.shape), _resident(wd.shape), _resident(cg.shape),
            _resident(pg.shape), _resident(pgs.shape),
            pl.BlockSpec((tm, LANES), lambda i: (i, 0)),
            pl.BlockSpec((tm, LANES), lambda i: (i, 0)),
            _resident(wu.shape), _resident(kg.shape),
        ],
        out_specs=[
            pl.BlockSpec((1, nh, tm, 2 * LANES), lambda i: (i // nt, 0, i % nt, 0)),
            pl.BlockSpec((1, nh, tm // tk, VT_ROWS, tk), lambda i: (i // nt, 0, i % nt, 0, 0)),
        ],
        out_shape=[
            jax.ShapeDtypeStruct((batch, nh, seq, 2 * LANES), BF16),
            jax.ShapeDtypeStruct((batch, nh, seq // tk, VT_ROWS, tk), BF16),
        ],
        compiler_params=_cparams(("parallel",)),
        name="kv_proj",
    )(h, g, wd, cg, pg, pgs, cos_t, sin_t, wu, kg)


def _b_proj_kernel(h_ref, g_ref, wdq_ref, cqg_ref, wuq_ref, gn_ref, gp_ref, gps_ref, cos_ref, sin_ref,
                   bd_ref, q_ref):
    nope_w = MLA_HEADS * MLA_QK_NOPE
    pe_w = MLA_HEADS * MLA_QK_ROPE
    hn = _rms_rows(h_ref[...], g_ref[...]).astype(BF16)
    cq = jnp.dot(hn, wdq_ref[...], preferred_element_type=F32)
    cqn = _rms_rows(cq, cqg_ref[...]).astype(BF16)
    q = jnp.dot(cqn, wuq_ref[...], preferred_element_type=F32)
    qp = q[:, nope_w:nope_w + pe_w]
    qps = q[:, nope_w + pe_w:]
    r = lax.rsqrt(_group_mean_sq(qp, bd_ref[...], MLA_QK_ROPE) + RMS_EPS)
    reps = pe_w // LANES
    cos_t = jnp.concatenate([cos_ref[...]] * reps, axis=1)
    sin_t = jnp.concatenate([sin_ref[...]] * reps, axis=1)
    rot = (qp * r * gp_ref[...]) * cos_t + (qps * r * gps_ref[...]) * sin_t
    lane = lax.broadcasted_iota(jnp.int32, (q.shape[0], LANES), 1)
    zero = jnp.zeros((q.shape[0], LANES), F32)
    for h in range(MLA_HEADS):
        qn = _rms_rows(q[:, h * LANES:(h + 1) * LANES], gn_ref[...])
        pair = rot[:, (h // 2) * LANES:(h // 2 + 1) * LANES]
        keep = (lane < MLA_QK_ROPE) if h % 2 == 0 else (lane >= MLA_QK_ROPE)
        q_ref[:, h * 2 * LANES:h * 2 * LANES + LANES] = qn.astype(BF16)
        q_ref[:, h * 2 * LANES + LANES:(h + 1) * 2 * LANES] = jnp.where(keep, pair, zero).astype(BF16)


def _b_proj(h, g, wdq, cqg, wuq, gn, gp, gps, cos_t, sin_t, bd):
    t, d = h.shape
    tm = PROJ_TM
    qw = MLA_HEADS * 2 * LANES
    return pl.pallas_call(
        _b_proj_kernel,
        grid=(t // tm,),
        in_specs=[
            pl.BlockSpec((tm, d), lambda i: (i, 0)),
            _resident(g.shape), _resident(wdq.shape), _resident(cqg.shape), _resident(wuq.shape),
            _resident(gn.shape), _resident(gp.shape), _resident(gps.shape),
            pl.BlockSpec((tm, LANES), lambda i: (i, 0)),
            pl.BlockSpec((tm, LANES), lambda i: (i, 0)),
            _resident(bd.shape),
        ],
        out_specs=pl.BlockSpec((tm, qw), lambda i: (i, 0)),
        out_shape=jax.ShapeDtypeStruct((t, qw), BF16),
        compiler_params=_cparams(("parallel",)),
        name="b_proj",
    )(h, g, wdq, cqg, wuq, gn, gp, gps, cos_t, sin_t, bd)


def _b_attn_kernel(q_ref, qn_ref, k_ref, vt_ref, mask_ref, o_ref, m_sc, acc_sc, s_sc, mb_sc, *, t, g):
    qi = pl.program_id(2)
    qw = 2 * LANES
    _flash_init(m_sc, acc_sc)

    th = t // 2
    region = {None: (slice(0, t), slice(0, t)), "a": (slice(0, th), slice(0, t)),
              "b": (slice(0, th), slice(th, t))}
    key0 = {None: 0, "a": 0, "b": th}

    def qk(slot, kj, kind, ref=q_ref):
        rows, cols = region[kind]
        nk = rows.stop - rows.start
        for gi in range(g):
            k = k_ref[0, gi, pl.ds(pl.multiple_of(kj * t + key0[kind], th), nk), :]
            _scores(k, ref[cols, gi * qw:(gi + 1) * qw], s_sc.at[slot, gi, rows, cols],
                    mb_sc.at[slot, gi, :, cols])

    @pl.when(qi == 0)
    def _():
        qk(0, 0, "a")

    def sm(slot, kj, kind):
        rows, cols = region[kind]
        keys = slice(key0[kind], key0[kind] + rows.stop - rows.start)
        for gi in range(g):
            extra = None if kind is None else mask_ref[:, 0:cols.stop - cols.start]
            _softmax_pv(s_sc.at[slot, gi, rows, cols], mb_sc.at[slot, gi, :, cols], extra,
                        vt_ref[0, gi, kj, :, keys], m_sc.at[gi, :, cols], acc_sc.at[gi, :, cols])

    odd = qi % 2 == 1
    _causal_pipeline(qi, qk, sm, [
        (jnp.logical_not(odd), [(qi, "a"), (qi, "b")]),
        (odd, [(qi - 1, None), (qi, "a"), (qi, "b")]),
    ])

    qk(0, 0, None, qn_ref)

    for gi in range(g):
        o_t = acc_sc[gi, :MLA_V_DIM] * (1.0 / acc_sc[gi, MLA_V_DIM:MLA_V_DIM + 1])
        o_ref[:, gi * MLA_V_DIM:(gi + 1) * MLA_V_DIM] = o_t.T.astype(BF16)


def _b_attn(q, k, vt, mask, *, batch, seq):
    nh = MLA_HEADS
    t = B_T
    g = B_G
    nq = seq // t
    return pl.pallas_call(
        functools.partial(_b_attn_kernel, t=t, g=g),
        grid=(batch, nh // g, nq),
        in_specs=[
            pl.BlockSpec((t, g * 2 * LANES), lambda b, h, i: (b * nq + i, h)),
            pl.BlockSpec((t, g * 2 * LANES), lambda b, h, i: (b * nq + jnp.minimum(i + 1, nq - 1), h)),
            pl.BlockSpec((1, g, seq, 2 * LANES), lambda b, h, i: (b, h, 0, 0)),
            pl.BlockSpec((1, g, nq, VT_ROWS, t), lambda b, h, i: (b, h, 0, 0, 0)),
            pl.BlockSpec(mask.shape, lambda b, h, i: (0, 0)),
        ],
        out_specs=pl.BlockSpec((t, g * MLA_V_DIM), lambda b, h, i: (b * nq + i, h)),
        out_shape=jax.ShapeDtypeStruct((batch * seq, nh * MLA_V_DIM), BF16),
        scratch_shapes=[
            pltpu.VMEM((g, 1, t), F32), pltpu.VMEM((g, VT_ROWS, t), F32),
            pltpu.VMEM((2, g, t, t), F32), pltpu.VMEM((2, g, 1, t), F32),
        ],
        compiler_params=_cparams(("arbitrary", "arbitrary", "arbitrary")),
        name="b_attn",
    )(q, q, k, vt, mask)


def _post_kernel(h_ref, o_ref, wo_ref, fg_ref, win_ref, cw_ref, cb_ref, wout_ref, pg_ref, wg_ref,
                 p_ref, wp_ref, out_ref, carry_sc, *, tiles_per_batch, d_ff):
    tm = h_ref.shape[0]
    ts = tm // 2
    halves = (slice(0, ts), slice(ts, tm))
    hist = F32_SUBLANES

    @pl.when((pl.program_id(0) % tiles_per_batch) == 0)
    def _():
        carry_sc[...] = jnp.zeros(carry_sc.shape, F32)

    h1 = [h_ref[r, :] + jnp.dot(o_ref[r, :], wo_ref[...], preferred_element_type=F32) for r in halves]
    hn = [_rms_rows(x, fg_ref[...]).astype(BF16) for x in h1]

    def up(lo, width):
        return [[jnp.dot(x, win_ref[:, c:c + width], preferred_element_type=F32)
                 for c in (lo, d_ff + lo)] for x in hn]

    def conv(u_halves, lo, width):
        prev = carry_sc[:, lo:lo + width]
        cw = cw_ref[:, lo:lo + width]
        outs = []
        for u in u_halves:
            ext = jnp.concatenate([prev, u], axis=0)
            outs.append(cb_ref[:, lo:lo + width] + ext[hist - 2:hist - 2 + ts] * cw[0:1]
                        + ext[hist - 1:hist - 1 + ts] * cw[1:2] + u * cw[2:3])
            prev = u[ts - hist:, :]
        carry_sc[:, lo:lo + width] = prev
        return outs

    chunks = [(lo, min(FF_CHUNK, d_ff - lo)) for lo in range(0, d_ff, FF_CHUNK)]
    u_next = up(*chunks[0])
    acc = [None, None]
    for ci, (lo, width) in enumerate(chunks):
        u = u_next
        if ci + 1 < len(chunks):
            u_next = up(*chunks[ci + 1])
        a = conv([u[0][0], u[1][0]], lo, width)
        gt = conv([u[0][1], u[1][1]], d_ff + lo, width)
        for i in range(2):
            act = (gt[i] * jax.nn.sigmoid(gt[i]) * a[i]).astype(BF16)
            part = jnp.dot(act, wout_ref[lo:lo + width, :], preferred_element_type=F32)
            acc[i] = part if acc[i] is None else acc[i] + part
    pe = [jnp.dot(p_ref[r, :].astype(BF16), wp_ref[...], preferred_element_type=F32) for r in halves]
    for i, r in enumerate(halves):
        h2 = h1[i] + acc[i]
        gate = jax.nn.sigmoid(jnp.dot(_rms_rows(h2, pg_ref[...]).astype(BF16), wg_ref[...],
                                      preferred_element_type=F32))
        out_ref[r, :] = h2 + pe[i] * gate


def _post(h, o, wo, fg, win, cw, cb, wout, pg, wg, p_all, wp, *, seq, layer):
    t, d = h.shape
    tm = POST_TM
    d_ff = wout.shape[0]
    return pl.pallas_call(
        functools.partial(_post_kernel, tiles_per_batch=seq // tm, d_ff=d_ff),
        grid=(t // tm,),
        in_specs=[
            pl.BlockSpec((tm, d), lambda i: (i, 0)),
            pl.BlockSpec((tm, o.shape[1]), lambda i: (i, 0)),
            _resident(wo.shape), _resident(fg.shape), _resident(win.shape), _resident(cw.shape),
            _resident(cb.shape), _resident(wout.shape), _resident(pg.shape), _resident(wg.shape),
            pl.BlockSpec((None, tm, p_all.shape[2]), lambda i: (layer, i, 0)),
            _resident(wp.shape),
        ],
        out_specs=pl.BlockSpec((tm, d), lambda i: (i, 0)),
        out_shape=jax.ShapeDtypeStruct((t, d), F32),
        scratch_shapes=[pltpu.VMEM((F32_SUBLANES, 2 * d_ff), F32)],
        compiler_params=_cparams(("arbitrary",)),
        name="post",
    )(h, o, wo, fg, win, cw, cb, wout, pg, wg, p_all, wp)


def _t5_bucket_table():
    n = np.arange(T5_MAX_DISTANCE + 1)
    max_exact = T5_BUCKETS // 2
    log_ratio = np.log(np.maximum(n, 1) / max_exact) / math.log(T5_MAX_DISTANCE / max_exact)
    large = np.minimum(max_exact + (log_ratio * (T5_BUCKETS - max_exact)).astype(np.int32),
                       T5_BUCKETS - 1)
    return np.where(n < max_exact, n, large).astype(np.int32)


def _bias_tiles(rel_table, t):
    assert t >= T5_MAX_DISTANCE
    nh = rel_table.shape[1]
    rel = rel_table[_t5_bucket_table()]
    rel = (rel - rel[T5_MAX_DISTANCE]).astype(F32).T
    near = rel[:, :T5_MAX_DISTANCE]
    zeros = jnp.zeros((nh, t - T5_MAX_DISTANCE), F32)

    def toeplitz(v_pos, v_neg):
        v = jnp.concatenate([v_pos, jnp.zeros((nh, 1), F32), v_neg], axis=1)
        u = jnp.tile(v, (1, t))[:, :t * (2 * t - 1)]
        return u.reshape(nh, t, 2 * t - 1)[:, :, :t]

    b0 = toeplitz(jnp.concatenate([near, zeros], axis=1), jnp.full((nh, t - 1), NEG_INF, F32))
    b1 = toeplitz(jnp.zeros((nh, t), F32), jnp.concatenate([near[:, 1:], zeros], axis=1))
    tiles = jnp.stack([b0, b1], axis=1)
    return jnp.concatenate([tiles, tiles], axis=-1)


def _causal_tile(t):
    kk = np.arange(t // 2)[:, None]
    qq = np.arange(t)[None, :]
    return jnp.asarray(np.where(qq >= kk, 0.0, NEG_INF), F32)


def _block_diag_ones(group):
    idx = np.arange(MXU_DIM) // group
    return jnp.asarray(idx[:, None] == idx[None, :], BF16)


def _swap_halves(v):
    half = v.shape[-1] // 2
    return jnp.concatenate([v[..., half:], v[..., :half]], axis=-1)


def kernel(x, p, positions, rel_bias_table, attn_norm, a_w_qkv, a_q_norm, a_k_norm, a_lam_q1, a_lam_k1, a_lam_q2, a_lam_k2, a_sub_norm, a_w_o, kv_norm, w_dkv, ckv_norm, w_ukv, k_nope_norm, k_pe_norm, b_w_dq, b_cq_norm, b_w_uq, b_q_nope_norm, b_q_pe_norm, b_w_o, ffn_norm, ffn_w_in, ffn_conv_w, ffn_conv_b, ffn_w_out, ple_norm, ple_w_proj, ple_w_gate):
    batch, seq, d = x.shape
    depth = p.shape[0]
    n_a = a_w_qkv.shape[0]
    t = batch * seq
    row = lambda v: v.reshape(1, -1).astype(F32)

    h = x.reshape(t, d)
    p_rows = p.reshape(depth, t, -1)
    bd64 = _block_diag_ones(DIFF_QK_DIM)
    bias = _bias_tiles(rel_bias_table, A_T) * LOG2E
    cos_t, sin_t = _rope_tables(positions.reshape(t, 1))
    shared = None

    for i in range(depth):
        if i < n_a:
            lam_init = 0.8 - 0.6 * math.exp(-0.3 * i)
            q_scale = DIFF_QK_DIM ** -0.5 * LOG2E
            gqk = jnp.concatenate([jnp.tile(a_q_norm[i], 2 * DIFF_HEADS) * q_scale,
                                   jnp.tile(a_k_norm[i], 2 * DIFF_HEADS)]).reshape(1, -1)
            q, k, vt = _a_proj(h, row(attn_norm[i]), a_w_qkv[i].astype(BF16), gqk, bd64,
                               batch=batch, seq=seq, tk=A_T)
            lam_params = jnp.stack([a_lam_q1[i], a_lam_k1[i], a_lam_q2[i], a_lam_k2[i]])
            o = _a_attn(q, k, vt, bias, lam_params, row(a_sub_norm[i]), lam_init=lam_init)
            o = o.reshape(t, -1)
            w_o = a_w_o[i]
        else:
            j = i - n_a
            if shared is None:
                lora = ckv_norm.shape[0]
                w_pe = w_dkv[:, lora:]
                w_pe_sw = _swap_halves(w_pe)
                wd = jnp.concatenate([w_dkv[:, :lora], w_pe, w_pe, w_pe_sw, w_pe_sw], axis=1)
                wu = w_ukv.reshape(lora, MLA_HEADS, MLA_QK_NOPE + MLA_V_DIM)
                wu = jnp.concatenate([wu[:, :, :MLA_QK_NOPE].reshape(lora, -1),
                                      wu[:, :, MLA_QK_NOPE:].reshape(lora, -1)], axis=1)
                shared = _kv_proj(h, row(kv_norm), wd.astype(BF16), row(ckv_norm),
                                  row(jnp.tile(k_pe_norm, 2)), row(jnp.tile(_swap_halves(k_pe_norm), 2)),
                                  cos_t, sin_t, wu.astype(BF16), row(k_nope_norm),
                                  batch=batch, seq=seq, tk=B_T)
                mask = _causal_tile(B_T)
            q_lora = b_w_uq.shape[1]
            wq = b_w_uq[j].reshape(q_lora, MLA_HEADS, MLA_QK_NOPE + MLA_QK_ROPE)
            wq_pe = wq[:, :, MLA_QK_NOPE:]
            wuq = jnp.concatenate([wq[:, :, :MLA_QK_NOPE].reshape(q_lora, -1),
                                   wq_pe.reshape(q_lora, -1),
                                   _swap_halves(wq_pe).reshape(q_lora, -1)], axis=1)
            scale = (MLA_QK_NOPE + MLA_QK_ROPE) ** -0.5 * LOG2E
            gp = b_q_pe_norm[j] * scale
            qf = _b_proj(h, row(attn_norm[i]), b_w_dq[j].astype(BF16), row(b_cq_norm[j]),
                         wuq.astype(BF16), row(b_q_nope_norm[j] * scale),
                         row(jnp.tile(gp, MLA_HEADS)), row(jnp.tile(_swap_halves(gp), MLA_HEADS)),
                         cos_t, sin_t, bd64)
            o = _b_attn(qf, shared[0], shared[1], mask, batch=batch, seq=seq)
            w_o = b_w_o[j]
        h = _post(h, o, w_o.astype(BF16), row(ffn_norm[i]), ffn_w_in[i].astype(BF16),
                  ffn_conv_w[i], row(ffn_conv_b[i]), ffn_w_out[i].astype(BF16), row(ple_norm[i]),
                  ple_w_gate[i].astype(BF16), p_rows, ple_w_proj[i].astype(BF16), seq=seq, layer=i)
    return h.reshape(batch, seq, d)
```
